```python
import math
import jax
import jax.numpy as jnp
from jax import lax
import numpy as np

D_MODEL = 1024
BATCH = 8
SEQ = 2048
DEPTH = 4
DEC_BATCH = 128
DEC_SEQ = 4
PAST_LEN = 2048
PAGE_SIZE = 128

N_MIXERS = 3
N_FOX_LAYERS = (DEPTH + 2) // 3
N_RWKV_LAYERS = (DEPTH + 1) // 3
N_MLSTM_LAYERS = DEPTH // 3
NORM_EPS = 1e-6
D_FF = 2816
FOX_HEADS = 16
FOX_HEAD_DIM = 64
FOX_WIDTH = FOX_HEADS * FOX_HEAD_DIM
Q_BLOCK = 128
FOX_FORGET_BIAS = 6.0
RWKV_HEAD_DIM = 64
RWKV_HEADS = D_MODEL // RWKV_HEAD_DIM
RWKV_DECAY_LORA = 64
RWKV_A_LORA = 64
RWKV_GATE_LORA = 160
RWKV_GN_EPS = 64e-5
MLSTM_HEADS = 4
MLSTM_V_DIM = D_MODEL // MLSTM_HEADS
MLSTM_QK_DIM = MLSTM_V_DIM // 2
MLSTM_CHUNK = 64
MLSTM_GATE_CAP = 15.0
MLSTM_FORGET_BIAS = 3.0

kernel_name = 'fox_rwkv7_mlstm_macaron_step'


def rmsnorm(x, g):
    xf = x.astype(jnp.float32)
    y = xf * lax.rsqrt(jnp.mean(xf * xf, axis=-1, keepdims=True) + NORM_EPS)
    return (y * g.astype(jnp.float32)).astype(x.dtype)


def swiglu(x, w_in, w_out):
    gate, up = jnp.split(x @ w_in, 2, axis=-1)
    return (jax.nn.silu(gate) * up) @ w_out


def fox_project(xn, w_in, b_f, g_q, g_k):
    B, T, _ = xn.shape
    W = FOX_WIDTH
    proj = xn @ w_in
    shp = (B, T, FOX_HEADS, FOX_HEAD_DIM)
    q = rmsnorm(proj[..., :W].reshape(shp), g_q)
    k = rmsnorm(proj[..., W:2 * W].reshape(shp), g_k)
    v = proj[..., 2 * W:3 * W].reshape(shp)
    gate = proj[..., 3 * W:4 * W].reshape(shp)
    logf = jax.nn.log_sigmoid((proj[..., 4 * W:] + b_f).astype(jnp.float32))
    return q, k, v, gate, logf


def fox_prompt(xn, w_in, b_f, g_q, g_k, w_out):
    B, T, _ = xn.shape
    q, k, v, gate, logf = fox_project(xn, w_in, b_f, g_q, g_k)
    dcum = jnp.cumsum(logf, axis=1)
    n_blk = T // Q_BLOCK
    q_blocks = jnp.moveaxis(q.reshape(B, n_blk, Q_BLOCK, FOX_HEADS, FOX_HEAD_DIM), 1, 0)
    d_blocks = jnp.moveaxis(dcum.reshape(B, n_blk, Q_BLOCK, FOX_HEADS), 1, 0)
    d_keys = jnp.swapaxes(dcum, 1, 2)[:, :, None, :]
    key_pos = jnp.arange(T)
    scale = FOX_HEAD_DIM ** -0.5

    def attend_block(args):
        q_blk, d_blk, blk = args
        s = jnp.einsum('bqhd,bkhd->bhqk', q_blk, k, preferred_element_type=jnp.float32) * scale
        s = s + jnp.swapaxes(d_blk, 1, 2)[..., None] - d_keys
        q_pos = blk * Q_BLOCK + jnp.arange(Q_BLOCK)
        s = jnp.where(key_pos[None, :] <= q_pos[:, None], s, -jnp.inf)
        p = jax.nn.softmax(s, axis=-1).astype(v.dtype)
        return jnp.einsum('bhqk,bkhd->bqhd', p, v)

    o = lax.map(attend_block, (q_blocks, d_blocks, jnp.arange(n_blk)))
    o = jnp.moveaxis(o, 0, 1).reshape(B, T, FOX_HEADS, FOX_HEAD_DIM)
    y = (o * jax.nn.sigmoid(gate)).reshape(B, T, FOX_WIDTH) @ w_out
    return y, k, v, logf


def fox_sample(xn, cache_k, cache_v, cache_f, page_table, w_in, b_f, g_q, g_k, w_out):
    B, T, _ = xn.shape
    q, k, v, gate, logf = fox_project(xn, w_in, b_f, g_q, g_k)
    past = page_table.shape[1] * cache_k.shape[1]
    k_past = cache_k[page_table].reshape(B, past, FOX_HEADS, FOX_HEAD_DIM)
    v_past = cache_v[page_table].reshape(B, past, FOX_HEADS, FOX_HEAD_DIM)
    f_past = cache_f[page_table].reshape(B, past, FOX_HEADS).astype(jnp.float32)
    suffix = lax.cumsum(f_past, axis=1, reverse=True) - f_past
    c_q = jnp.swapaxes(jnp.cumsum(logf, axis=1), 1, 2)
    scale = FOX_HEAD_DIM ** -0.5
    s_past = jnp.einsum('bqhd,bkhd->bhqk', q, k_past, preferred_element_type=jnp.float32) * scale
    s_past = s_past + c_q[..., None] + jnp.swapaxes(suffix, 1, 2)[:, :, None, :]
    s_new = jnp.einsum('bqhd,bkhd->bhqk', q, k, preferred_element_type=jnp.float32) * scale
    s_new = s_new + c_q[..., None] - c_q[:, :, None, :]
    s_new = jnp.where(jnp.tril(jnp.ones((T, T), dtype=bool)), s_new, -jnp.inf)
    p = jax.nn.softmax(jnp.concatenate([s_past, s_new], axis=-1), axis=-1).astype(v.dtype)
    o = (jnp.einsum('bhqk,bkhd->bqhd', p[..., :past], v_past)
         + jnp.einsum('bhqk,bkhd->bqhd', p[..., past:], v))
    y = (o * jax.nn.sigmoid(gate)).reshape(B, T, FOX_WIDTH) @ w_out
    return y, k, v, logf


def rwkv7_mix(xn, shift_prev, s0, mu, w0, w1, w2, a0, a1, a2, g1, g2, k_k, k_a, r_k,
              w_rkv, w_o, gn_w, gn_b):
    B, T, D = xn.shape
    H, N = RWKV_HEADS, RWKV_HEAD_DIM
    f32 = jnp.float32
    x_prev = jnp.concatenate([shift_prev[:, None, :].astype(xn.dtype), xn[:, :-1]], axis=1)
    xx = x_prev - xn
    xr, xw, xk, xv, xa, xg = [xn + xx * mu[i] for i in range(6)]
    r = xr @ w_rkv[0]
    k = xk @ w_rkv[1]
    v = xv @ w_rkv[2]
    w_log = -jax.nn.softplus(-(w0 + jnp.tanh(xw @ w1) @ w2).astype(f32)) - 0.5
    decay = jnp.exp(-jnp.exp(w_log))
    a = jax.nn.sigmoid(a0 + (xa @ a1) @ a2)
    g = jax.nn.sigmoid(xg @ g1) @ g2
    kk = (k * k_k).reshape(B, T, H, N).astype(f32)
    kk = kk / jnp.maximum(jnp.sqrt(jnp.sum(kk * kk, axis=-1, keepdims=True)), 1e-12)
    k = k * (1 + (a - 1) * k_a)
    heads = lambda t: t.reshape(B, T, H, N)
    r_h, k_h, v_h, a_h = heads(r), heads(k), heads(v), heads(a)
    aa = -kk
    bb = kk * a_h.astype(f32)

    def step(S, inp):
        r_t, w_t, k_t, v_t, a_t, b_t = inp
        sa = jnp.einsum('bhij,bhj->bhi', S, a_t)
        S = S * w_t[:, :, None, :] + sa[..., None] * b_t[:, :, None, :] + v_t[..., None] * k_t[:, :, None, :]
        return S, jnp.einsum('bhij,bhj->bhi', S, r_t)

    tm = lambda t: jnp.swapaxes(t, 0, 1).astype(f32)
    s_T, y = lax.scan(step, s0.astype(f32),
                      (tm(r_h), tm(heads(decay)), tm(k_h), tm(v_h), tm(aa), tm(bb)))
    y = jnp.swapaxes(y, 0, 1)
    mean = jnp.mean(y, axis=-1, keepdims=True)
    var = jnp.mean(jnp.square(y - mean), axis=-1, keepdims=True)
    y = ((y - mean) * lax.rsqrt(var + RWKV_GN_EPS)).reshape(B, T, D) * gn_w + gn_b
    bonus = jnp.sum(r_h * k_h * r_k, axis=-1, keepdims=True) * v_h
    y = y.astype(xn.dtype) + bonus.reshape(B, T, D)
    return (y * g) @ w_o, xn[:, -1], s_T


def mlstm_mix(xn, c0, n0, m0, w_in, b_if, g_h, w_out):
    B, T, _ = xn.shape
    H, DK, DV = MLSTM_HEADS, MLSTM_QK_DIM, MLSTM_V_DIM
    f32 = jnp.float32
    proj = xn @ w_in
    o1, o2 = H * DK, 2 * H * DK
    o3, o4 = o2 + H * DV, o2 + 2 * H * DV
    q = proj[..., :o1].reshape(B, T, H, DK).astype(f32) * DK ** -0.5
    k = proj[..., o1:o2].reshape(B, T, H, DK).astype(f32)
    v = proj[..., o2:o3].reshape(B, T, H, DV).astype(f32)
    og = proj[..., o3:o4]
    gates = (proj[..., o4:] + b_if).astype(f32)
    gates = MLSTM_GATE_CAP * jnp.tanh(gates / MLSTM_GATE_CAP)
    log_i = gates[..., :H]
    log_f = jax.nn.log_sigmoid(gates[..., H:])
    L = math.gcd(T, MLSTM_CHUNK)
    n_chunk = T // L
    chunks = lambda t: jnp.moveaxis(t.reshape((B, n_chunk, L) + t.shape[2:]), 1, 0)
    tril = jnp.tril(jnp.ones((L, L), dtype=bool))

    def chunk_step(carry, inp):
        C, n, m = carry
        qc, kc, vc, lic, lfc = inp
        bT = jnp.swapaxes(jnp.cumsum(lfc, axis=1), 1, 2)
        liT = jnp.swapaxes(lic, 1, 2)
        logw = bT[..., :, None] - bT[..., None, :] + liT[..., None, :]
        logw = jnp.where(tril, logw, -jnp.inf)
        inter = m[..., None] + bT
        m_t = jnp.maximum(inter, jnp.max(logw, axis=-1))
        wts = jnp.exp(logw - m_t[..., None])
        inter_scale = jnp.exp(inter - m_t)
        sw = wts * jnp.einsum('blhd,bshd->bhls', qc, kc)
        num = (inter_scale[..., None] * jnp.einsum('bhvk,blhk->bhlv', C, qc)
               + jnp.einsum('bhls,bshv->bhlv', sw, vc))
        den = inter_scale * jnp.einsum('bhk,blhk->bhl', n, qc) + jnp.sum(sw, axis=-1)
        h = num / jnp.maximum(jnp.abs(den), jnp.exp(-m_t))[..., None]
        m_new = m_t[..., -1]
        carry_scale = jnp.exp(inter[..., -1] - m_new)
        w_s = jnp.exp(bT[..., -1:] - bT + liT - m_new[..., None])
        C_new = carry_scale[..., None, None] * C + jnp.einsum('bhs,bshv,bshk->bhvk', w_s, vc, kc)
        n_new = carry_scale[..., None] * n + jnp.einsum('bhs,bshk->bhk', w_s, kc)
        return (C_new, n_new, m_new), jnp.swapaxes(h, 1, 2)

    (c_T, n_T, m_T), h = lax.scan(chunk_step, (c0.astype(f32), n0.astype(f32), m0.astype(f32)),
                                  (chunks(q), chunks(k), chunks(v), chunks(log_i), chunks(log_f)))
    h = jnp.moveaxis(h, 0, 1).reshape(B, T, H, DV)
    h = h * lax.rsqrt(jnp.mean(h * h, axis=-1, keepdims=True) + NORM_EPS)
    h = (h.reshape(B, T, H * DV) * g_h).astype(xn.dtype) * jax.nn.sigmoid(og)
    return h @ w_out, c_T, n_T, m_T


def setup_inputs(seed: int = 0) -> dict:
    key = jax.random.key(seed)
    keys = iter(jax.random.split(key, 64))
    f32 = jnp.float32

    def normal(shape, scale=1.0):
        return scale * jax.random.normal(next(keys), shape, f32)

    def dense(shape):
        return normal(shape, shape[-2] ** -0.5)

    def gain(shape):
        return 1.0 + normal(shape, 0.02)

    n_pages = PAST_LEN // PAGE_SIZE
    n_used = DEC_BATCH * n_pages
    n_pool = n_used + n_used // 4
    kv_shape = (n_pool, PAGE_SIZE, FOX_HEADS, FOX_HEAD_DIM)
    f_shape = (n_pool, PAGE_SIZE, FOX_HEADS)
    HM, DK, DV = MLSTM_HEADS, MLSTM_QK_DIM, MLSTM_V_DIM
    NF, NR, NM = N_FOX_LAYERS, N_RWKV_LAYERS, N_MLSTM_LAYERS
    D = D_MODEL
    inp = {}
    inp['x_prompt'] = normal((BATCH, SEQ, D))
    inp['x_sample'] = normal((DEC_BATCH, DEC_SEQ, D))
    inp['cache_k_l0'] = normal(kv_shape)
    inp['cache_v_l0'] = normal(kv_shape)
    inp['cache_f_l0'] = jax.nn.log_sigmoid(FOX_FORGET_BIAS + normal(f_shape, 0.5))
    inp['state_shift_l1'] = normal((DEC_BATCH, D))
    inp['state_wkv_l1'] = normal((DEC_BATCH, RWKV_HEADS, RWKV_HEAD_DIM, RWKV_HEAD_DIM), 0.5)
    inp['state_c_l2'] = normal((DEC_BATCH, HM, DV, DK))
    inp['state_n_l2'] = normal((DEC_BATCH, HM, DK))
    inp['state_m_l2'] = normal((DEC_BATCH, HM))
    inp['cache_k_l3'] = normal(kv_shape)
    inp['cache_v_l3'] = normal(kv_shape)
    inp['cache_f_l3'] = jax.nn.log_sigmoid(FOX_FORGET_BIAS + normal(f_shape, 0.5))
    inp['page_table'] = jax.random.permutation(next(keys), n_pool)[:n_used].reshape(
        DEC_BATCH, n_pages).astype(jnp.int32)
    inp['norm_g'] = gain((DEPTH, 3, D))
    inp['final_g'] = gain((D,))
    inp['ffn_w_in'] = dense((DEPTH, 2, D, 2 * D_FF))
    inp['ffn_w_out'] = dense((DEPTH, 2, D_FF, D))
    inp['fox_w_in'] = dense((NF, D, 4 * FOX_WIDTH + FOX_HEADS))
    inp['fox_b_f'] = FOX_FORGET_BIAS + normal((NF, FOX_HEADS), 0.1)
    inp['fox_gq'] = gain((NF, FOX_HEAD_DIM))
    inp['fox_gk'] = gain((NF, FOX_HEAD_DIM))
    inp['fox_w_out'] = dense((NF, FOX_WIDTH, D))
    inp['rw_mu'] = jax.random.uniform(next(keys), (NR, 6, D), f32)
    inp['rw_w0'] = normal((NR, D), 0.5)
    inp['rw_w1'] = dense((NR, D, RWKV_DECAY_LORA))
    inp['rw_w2'] = dense((NR, RWKV_DECAY_LORA, D))
    inp['rw_a0'] = normal((NR, D), 0.1)
    inp['rw_a1'] = dense((NR, D, RWKV_A_LORA))
    inp['rw_a2'] = dense((NR, RWKV_A_LORA, D))
    inp['rw_g1'] = dense((NR, D, RWKV_GATE_LORA))
    inp['rw_g2'] = dense((NR, RWKV_GATE_LORA, D))
    inp['rw_kk'] = gain((NR, D))
    inp['rw_ka'] = gain((NR, D))
    inp['rw_rk'] = normal((NR, RWKV_HEADS, RWKV_HEAD_DIM), 0.1)
    inp['rw_w_rkv'] = dense((NR, 3, D, D))
    inp['rw_w_o'] = dense((NR, D, D))
    inp['rw_gn_w'] = gain((NR, D))
    inp['rw_gn_b'] = normal((NR, D), 0.02)
    inp['ml_w_in'] = dense((NM, D, 2 * HM * DK + 2 * HM * DV + 2 * HM))
    inp['ml_b_if'] = jnp.concatenate([normal((NM, HM), 0.1),
                                      MLSTM_FORGET_BIAS + normal((NM, HM), 0.1)], axis=-1)
    inp['ml_gn'] = gain((NM, HM * DV))
    inp['ml_w_out'] = dense((NM, HM * DV, D))
    return inp


def reference(x_prompt, x_sample,
              cache_k_l0, cache_v_l0, cache_f_l0,
              state_shift_l1, state_wkv_l1,
              state_c_l2, state_n_l2, state_m_l2,
              cache_k_l3, cache_v_l3, cache_f_l3,
              page_table,
              norm_g, final_g, ffn_w_in, ffn_w_out,
              fox_w_in, fox_b_f, fox_gq, fox_gk, fox_w_out,
              rw_mu, rw_w0, rw_w1, rw_w2, rw_a0, rw_a1, rw_a2, rw_g1, rw_g2,
              rw_kk, rw_ka, rw_rk, rw_w_rkv, rw_w_o, rw_gn_w, rw_gn_b,
              ml_w_in, ml_b_if, ml_gn, ml_w_out):
    xp, xs = x_prompt, x_sample
    bp = xp.shape[0]
    fox_caches = ((cache_k_l0, cache_v_l0, cache_f_l0), (cache_k_l3, cache_v_l3, cache_f_l3))
    rwkv_states = ((state_shift_l1, state_wkv_l1),)
    mlstm_states = ((state_c_l2, state_n_l2, state_m_l2),)
    layer_state = []
    for i in range(DEPTH):
        kind, j = i % N_MIXERS, i // N_MIXERS
        xp = xp + 0.5 * swiglu(rmsnorm(xp, norm_g[i, 0]), ffn_w_in[i, 0], ffn_w_out[i, 0])
        xs = xs + 0.5 * swiglu(rmsnorm(xs, norm_g[i, 0]), ffn_w_in[i, 0], ffn_w_out[i, 0])
        hp, hs = rmsnorm(xp, norm_g[i, 1]), rmsnorm(xs, norm_g[i, 1])
        if kind == 0:
            fw = (fox_w_in[j], fox_b_f[j], fox_gq[j], fox_gk[j], fox_w_out[j])
            yp, kp, vp, fp = fox_prompt(hp, *fw)
            ck, cv, cf = fox_caches[j]
            ys, ks, vs, fs = fox_sample(hs, ck, cv, cf, page_table, *fw)
            layer_state.append((kp, vp, fp, ks, vs, fs))
        elif kind == 1:
            rw = (rw_mu[j], rw_w0[j], rw_w1[j], rw_w2[j], rw_a0[j], rw_a1[j], rw_a2[j],
                  rw_g1[j], rw_g2[j], rw_kk[j], rw_ka[j], rw_rk[j], rw_w_rkv[j], rw_w_o[j],
                  rw_gn_w[j], rw_gn_b[j])
            sh0 = jnp.zeros((bp, D_MODEL), hp.dtype)
            wkv0 = jnp.zeros((bp, RWKV_HEADS, RWKV_HEAD_DIM, RWKV_HEAD_DIM), jnp.float32)
            yp, shp, sp = rwkv7_mix(hp, sh0, wkv0, *rw)
            sh_in, wkv_in = rwkv_states[j]
            ys, shs, ss = rwkv7_mix(hs, sh_in, wkv_in, *rw)
            layer_state.append((shp, sp, shs, ss))
        else:
            mw = (ml_w_in[j], ml_b_if[j], ml_gn[j], ml_w_out[j])
            c0 = jnp.zeros((bp, MLSTM_HEADS, MLSTM_V_DIM, MLSTM_QK_DIM), jnp.float32)
            n0 = jnp.zeros((bp, MLSTM_HEADS, MLSTM_QK_DIM), jnp.float32)
            m0 = jnp.zeros((bp, MLSTM_HEADS), jnp.float32)
            yp, cp, np_, mp = mlstm_mix(hp, c0, n0, m0, *mw)
            c_in, n_in, m_in = mlstm_states[j]
            ys, cs, ns, ms = mlstm_mix(hs, c_in, n_in, m_in, *mw)
            layer_state.append((cp, np_, mp, cs, ns, ms))
        xp = xp + yp
        xs = xs + ys
        xp = xp + 0.5 * swiglu(rmsnorm(xp, norm_g[i, 2]), ffn_w_in[i, 1], ffn_w_out[i, 1])
        xs = xs + 0.5 * swiglu(rmsnorm(xs, norm_g[i, 2]), ffn_w_in[i, 1], ffn_w_out[i, 1])
    y_prompt = rmsnorm(xp, final_g)
    y_sample = rmsnorm(xs, final_g)
    ((k_prompt_l0, v_prompt_l0, f_prompt_l0, k_sample_l0, v_sample_l0, f_sample_l0),
     (shift_prompt_l1, wkv_prompt_l1, shift_sample_l1, wkv_sample_l1),
     (c_prompt_l2, n_prompt_l2, m_prompt_l2, c_sample_l2, n_sample_l2, m_sample_l2),
     (k_prompt_l3, v_prompt_l3, f_prompt_l3, k_sample_l3, v_sample_l3, f_sample_l3)) = layer_state
    return (y_prompt, y_sample,
            k_prompt_l0, v_prompt_l0, f_prompt_l0, k_sample_l0, v_sample_l0, f_sample_l0,
            shift_prompt_l1, wkv_prompt_l1, shift_sample_l1, wkv_sample_l1,
            c_prompt_l2, n_prompt_l2, m_prompt_l2, c_sample_l2, n_sample_l2, m_sample_l2,
            k_prompt_l3, v_prompt_l3, f_prompt_l3, k_sample_l3, v_sample_l3, f_sample_l3)
```

```python
import functools
import math

import jax
import jax.numpy as jnp
from jax import lax
from jax.experimental import pallas as pl
from jax.experimental.pallas import tpu as pltpu

F32 = jnp.float32
BF16 = jnp.bfloat16

NORM_EPS = 1e-6
VMEM_LIMIT_BYTES = 56 * 1024 * 1024
ROW_TILE = 512
LANES = 128
SUBLANES = 8


def _params(*semantics):
    return pltpu.CompilerParams(dimension_semantics=semantics, vmem_limit_bytes=VMEM_LIMIT_BYTES)


def _const_spec(shape):
    return pl.BlockSpec(shape, lambda *_: (0,) * len(shape))


def _rmsnorm(x, g):
    ms = jnp.mean(x * x, axis=-1, keepdims=True)
    return x * lax.rsqrt(ms + NORM_EPS) * g


def _dot(a, b):
    return jnp.dot(a, b, preferred_element_type=F32)


def _dot_nt(a, b):
    return lax.dot_general(a, b, (((1,), (1,)), ((), ())), preferred_element_type=F32)


def _dot_tn(a, b):
    return lax.dot_general(a, b, (((0,), (0,)), ((), ())), preferred_element_type=F32)


FFN_CHUNK = 256


def _ffn_body(rows, consts, outs, is_sample, tile_index):
    (x_ref,), (g_ref, win_ref, wout_ref), (o_ref,) = rows, consts, outs
    d_ff = wout_ref.shape[0]
    x = x_ref[...]
    xn = _rmsnorm(x, g_ref[...]).astype(BF16)
    acc = jnp.zeros_like(x)
    for c in range(d_ff // FFN_CHUNK):
        lo = c * FFN_CHUNK
        gate = _dot(xn, win_ref[:, lo:lo + FFN_CHUNK])
        up = _dot(xn, win_ref[:, d_ff + lo:d_ff + lo + FFN_CHUNK])
        act = (gate * jax.nn.sigmoid(gate) * up).astype(BF16)
        acc = acc + _dot(act, wout_ref[lo:lo + FFN_CHUNK, :])
    o_ref[...] = x + 0.5 * acc


def _ffn(x, g, w_in, w_out):
    d = w_in.shape[0]
    return _row_call(_ffn_body, [x], [g.reshape(1, d), w_in, w_out], [(d, F32, False)], "ffn")[0]


def _row_call(body, rows, consts, outs, name, tile_div=1):
    n_p, n_s = rows[0][0].shape[0], rows[0][1].shape[0]
    tile = ROW_TILE // tile_div
    assert n_p % tile == 0 and n_s % tile == 0
    tp, ts = n_p // tile, n_s // tile
    p_idx = lambda i: jnp.minimum(i, tp - 1)
    s_idx = lambda i: jnp.maximum(i - tp, 0)
    per_tile = tile // SUBLANES
    prev_of = lambda t: jnp.maximum(t * per_tile - 1, 0)
    in_specs, args, row_slots = [], [], []
    for entry in rows:
        prev = len(entry) == 3
        slots = []
        for a, idx in ((entry[0], p_idx), (entry[1], s_idx)):
            if a is None:
                slots.append(None)
                continue
            if prev:
                in_specs.append(pl.BlockSpec((SUBLANES, a.shape[1]), lambda i, idx=idx: (prev_of(idx(i)), 0)))
            else:
                in_specs.append(pl.BlockSpec((tile, a.shape[1]), lambda i, idx=idx: (idx(i), 0)))
            slots.append(len(args))
            args.append(a)
        row_slots.append(slots)
    n_row_args = len(args)
    for c in consts:
        in_specs.append(_const_spec(c.shape))
        args.append(c)
    out_specs, out_shape, out_slots = [], [], []
    for cols, dtype, kind in outs:
        slots = []
        if kind == "whole":
            out_specs.append(_const_spec(cols))
            out_shape.append(jax.ShapeDtypeStruct(cols, dtype))
            slots = [len(out_shape) - 1] * 2
        else:
            for n, idx, absent in ((n_p, p_idx, kind == "sample"), (n_s, s_idx, False)):
                if absent:
                    slots.append(None)
                    continue
                if kind is True:
                    out_specs.append(pl.BlockSpec((cols, tile), lambda i, idx=idx: (0, idx(i))))
                    out_shape.append(jax.ShapeDtypeStruct((cols, n), dtype))
                else:
                    out_specs.append(pl.BlockSpec((tile, cols), lambda i, idx=idx: (idx(i), 0)))
                    out_shape.append(jax.ShapeDtypeStruct((n, cols), dtype))
                slots.append(len(out_shape) - 1)
        out_slots.append(slots)
    nc = len(consts)

    def kern(*refs):
        i = pl.program_id(0)
        r, c, o = refs[:n_row_args], refs[n_row_args:n_row_args + nc], refs[n_row_args + nc:]
        pick = lambda seq, slots, side: [None if s[side] is None else seq[s[side]] for s in slots]

        @pl.when(i < tp)
        def _():
            body(pick(r, row_slots, 0), c, pick(o, out_slots, 0), False, i)

        @pl.when(i >= tp)
        def _():
            body(pick(r, row_slots, 1), c, pick(o, out_slots, 1), True, i - tp)

    res = pl.pallas_call(
        kern, grid=(tp + ts,), in_specs=in_specs, out_specs=out_specs, out_shape=out_shape,
        compiler_params=_params("arbitrary"), name=name,
    )(*args)
    result = []
    for (cols, dtype, kind), slots in zip(outs, out_slots):
        if kind == "whole":
            result.append(res[slots[0]])
        else:
            result.append(tuple(None if s is None else res[s] for s in slots))
    return result


def _split3(x):
    hi = x.astype(BF16)
    r1 = x - hi.astype(F32)
    mid = r1.astype(BF16)
    lo = (r1 - mid.astype(F32)).astype(BF16)
    return hi, mid, lo


def _dot01(m01, x):
    hi, mid, lo = _split3(x)
    return _dot(m01, hi) + _dot(m01, mid) + _dot(m01, lo)


def _dot01_r(x, m01):
    hi, mid, lo = _split3(x)
    return _dot(hi, m01) + _dot(mid, m01) + _dot(lo, m01)


def _log_sigmoid(x):
    return jnp.minimum(x, 0.0) - jnp.log1p(jnp.exp(-jnp.abs(x)))


FOX_HEADS = 16
FOX_HEAD_DIM = 64
FOX_WIDTH = FOX_HEADS * FOX_HEAD_DIM
FOX_SCALE = FOX_HEAD_DIM ** -0.5


def _fox_proj_body(rows, consts, outs, is_sample, tile_index):
    (x_ref,) = rows
    g_ref, w_ref, wf_ref, bf_ref = consts
    q_ref, k_ref, v_ref, gate_ref, f_ref, ft_ref = outs
    xn = _rmsnorm(x_ref[...], g_ref[...]).astype(BF16)
    for j, o_ref in enumerate((q_ref, k_ref, v_ref, gate_ref)):
        o_ref[...] = _dot(xn, w_ref[:, j * FOX_WIDTH:(j + 1) * FOX_WIDTH])
    logf = _log_sigmoid(_dot(xn, wf_ref[...]) + bf_ref[...])
    f_ref[...] = logf[:, :FOX_HEADS]
    ft_ref[...] = logf.T[:FOX_HEADS, :]


def _fox_project(x, g, w_in, b_f):
    w_main = w_in[:, :4 * FOX_WIDTH].astype(BF16)
    w_f = jnp.pad(w_in[:, 4 * FOX_WIDTH:], ((0, 0), (0, LANES - FOX_HEADS))).astype(BF16)
    b = jnp.pad(b_f, (0, LANES - FOX_HEADS)).reshape(1, LANES)
    outs = [(FOX_WIDTH, F32, False)] * 4 + [(FOX_HEADS, F32, False), (FOX_HEADS, F32, True)]
    return _row_call(_fox_proj_body, [x], [g.reshape(1, -1), w_main, w_f, b], outs, "fox_proj")


def _fox_cumsum_kernel(f_ref, ft_ref, d_ref, dt_ref, *, seq):
    r = lax.broadcasted_iota(jnp.int32, (LANES, LANES), 0)
    c = lax.broadcasted_iota(jnp.int32, (LANES, LANES), 1)
    lower = jnp.where(c <= r, 1.0, 0.0).astype(BF16)
    upper = jnp.where(r <= c, 1.0, 0.0).astype(BF16)
    carry_c = jnp.zeros((1, FOX_HEADS), F32)
    carry_r = jnp.zeros((FOX_HEADS, 1), F32)
    for blk in range(seq // LANES):
        sl = slice(blk * LANES, (blk + 1) * LANES)
        loc = _dot01(lower, f_ref[sl, :]) + carry_c
        d_ref[sl, :] = loc
        carry_c = loc[LANES - 1:LANES, :]
        loc_r = _dot01_r(ft_ref[:, sl], upper) + carry_r
        dt_ref[:, sl] = loc_r
        carry_r = loc_r[:, LANES - 1:LANES]


def _fox_cumsum(f, ft, batch, seq):
    return pl.pallas_call(
        functools.partial(_fox_cumsum_kernel, seq=seq),
        grid=(batch,),
        in_specs=[pl.BlockSpec((seq, FOX_HEADS), lambda b: (b, 0)),
                  pl.BlockSpec((FOX_HEADS, seq), lambda b: (0, b))],
        out_specs=[pl.BlockSpec((seq, FOX_HEADS), lambda b: (b, 0)),
                   pl.BlockSpec((FOX_HEADS, seq), lambda b: (0, b))],
        out_shape=[jax.ShapeDtypeStruct(f.shape, F32), jax.ShapeDtypeStruct(ft.shape, F32)],
        compiler_params=_params("parallel"),
        name="fox_cumsum",
    )(f, ft)


def _pair_head_norm(x, g, first_head):
    x2 = x * x
    s0 = jnp.sum(jnp.where(first_head, x2, 0.0), axis=-1, keepdims=True)
    s1 = jnp.sum(jnp.where(first_head, 0.0, x2), axis=-1, keepdims=True)
    inv = jnp.where(first_head, lax.rsqrt(s0 / FOX_HEAD_DIM + NORM_EPS), lax.rsqrt(s1 / FOX_HEAD_DIM + NORM_EPS))
    return x * inv * g


def _fox_prompt_kernel(q_ref, k_ref, v_ref, d_ref, dt_ref, gq_ref, gk_ref, o_ref, kn_ref, k_sc, v_sc, *, tq):
    pair = pl.program_id(1)
    qi = pl.program_id(2)
    first_head = lax.broadcasted_iota(jnp.int32, (1, LANES), 1) < FOX_HEAD_DIM

    @pl.when(qi == 0)
    def _():
        kn = _pair_head_norm(k_ref[...], gk_ref[...], first_head)
        kn_ref[...] = kn
        k_sc[...] = kn.astype(BF16)
        v_sc[...] = v_ref[...].astype(BF16)

    qn = _pair_head_norm(q_ref[...], gq_ref[...], first_head) * FOX_SCALE
    q_heads = (jnp.where(first_head, qn, 0.0).astype(BF16), jnp.where(first_head, 0.0, qn).astype(BF16))
    d_blk = d_ref[...]
    head_lane = lax.broadcasted_iota(jnp.int32, (1, FOX_HEADS), 1)
    d_q = [jnp.sum(jnp.where(head_lane == 2 * pair + e, d_blk, 0.0), axis=-1, keepdims=True) for e in (0, 1)]
    row = lax.broadcasted_iota(jnp.int32, (tq, tq), 0)
    col = lax.broadcasted_iota(jnp.int32, (tq, tq), 1)

    def kv_block(j, carry, diagonal):
        start = pl.multiple_of(j * tq, tq)
        kb = k_sc[pl.ds(start, tq), :]
        vb = v_sc[pl.ds(start, tq), :]
        new = []
        for e in (0, 1):
            m, l, acc = carry[e]
            d_k = dt_ref[pl.ds(2 * pair + e, 1), pl.ds(start, tq)]
            s = _dot_nt(q_heads[e], kb) + (d_q[e] - d_k)
            if diagonal:
                s = jnp.where(col <= row, s, -jnp.inf)
            m_new = jnp.maximum(m, jnp.max(s, axis=-1, keepdims=True))
            alpha = jnp.exp(m - m_new)
            p = jnp.exp(s - m_new)
            l = alpha * l + jnp.sum(p, axis=-1, keepdims=True)
            acc = alpha * acc + _dot(p.astype(BF16), vb)
            new.append((m_new, l, acc))
        return tuple(new)

    init = tuple((jnp.full((tq, 1), -jnp.inf, F32), jnp.zeros((tq, 1), F32), jnp.zeros((tq, LANES), F32))
                 for _ in (0, 1))
    carry = lax.fori_loop(0, qi, lambda j, c: kv_block(j, c, False), init)
    (_, l0, acc0), (_, l1, acc1) = kv_block(qi, carry, True)
    o_ref[...] = jnp.where(first_head, acc0 / l0, acc1 / l1)


FOX_Q_BLOCK = 512


def _fox_prompt_attention(q, k, v, dcum, dcum_t, gq, gk, batch, seq):
    tq = FOX_Q_BLOCK
    nq = seq // tq
    n = batch * seq
    pairs = FOX_WIDTH // LANES
    gq2 = jnp.tile(gq, 2).reshape(1, LANES)
    gk2 = jnp.tile(gk, 2).reshape(1, LANES)
    return pl.pallas_call(
        functools.partial(_fox_prompt_kernel, tq=tq),
        grid=(batch, pairs, nq),
        in_specs=[
            pl.BlockSpec((tq, LANES), lambda b, p, i: (b * nq + i, p)),
            pl.BlockSpec((seq, LANES), lambda b, p, i: (b, p)),
            pl.BlockSpec((seq, LANES), lambda b, p, i: (b, p)),
            pl.BlockSpec((tq, FOX_HEADS), lambda b, p, i: (b * nq + i, 0)),
            pl.BlockSpec((FOX_HEADS, seq), lambda b, p, i: (0, b)),
            _const_spec((1, LANES)),
            _const_spec((1, LANES)),
        ],
        out_specs=[
            pl.BlockSpec((tq, LANES), lambda b, p, i: (b * nq + i, p)),
            pl.BlockSpec((seq, LANES), lambda b, p, i: (b, p)),
        ],
        out_shape=[jax.ShapeDtypeStruct((n, FOX_WIDTH), F32), jax.ShapeDtypeStruct((n, FOX_WIDTH), F32)],
        scratch_shapes=[pltpu.VMEM((seq, LANES), BF16), pltpu.VMEM((seq, LANES), BF16)],
        compiler_params=_params("parallel", "parallel", "arbitrary"),
        name="fox_prompt_attention",
    )(q, k, v, dcum, dcum_t, gq2, gk2)


FOX_PAGE_GROUP = 4


def _fox_sample_kernel(pt_ref, q_ref, k_ref, v_ref, f_ref, gq_ref, gk_ref, *rest, group, n_groups, t_new):
    kp, vp, fp = rest[:group], rest[group:2 * group], rest[2 * group:3 * group]
    o_ref, kn_ref = rest[3 * group:3 * group + 2]
    q_sc, kx_sc, vx_sc, cq_sc, ck_sc, s_sc, m_sc, l_sc, acc_sc, carry_sc = rest[3 * group + 2:]
    step = pl.program_id(1)
    hd, rows_h = FOX_HEAD_DIM, SUBLANES
    page = kp[0].shape[1]

    @pl.when(step == 0)
    def _():
        q, k, v, f = q_ref[0], k_ref[0], v_ref[0], f_ref[0]
        c_rows = [f[0:1, :]]
        for t in range(1, t_new):
            c_rows.append(c_rows[-1] + f[t:t + 1, :])
        c = jnp.concatenate(c_rows, axis=0)
        pad = jnp.zeros((rows_h - t_new, hd), F32)
        for h in range(FOX_HEADS):
            cs = slice(h * hd, (h + 1) * hd)
            rs = slice(h * rows_h, (h + 1) * rows_h)
            qh, kh = q[:, cs], k[:, cs]
            qn = qh * lax.rsqrt(jnp.mean(qh * qh, axis=-1, keepdims=True) + NORM_EPS) * gq_ref[...] * FOX_SCALE
            kn = kh * lax.rsqrt(jnp.mean(kh * kh, axis=-1, keepdims=True) + NORM_EPS) * gk_ref[...]
            kn_ref[0, :, cs] = kn
            q_sc[rs, :] = jnp.concatenate([qn, pad], axis=0)
            cq_sc[rs, :] = jnp.broadcast_to(
                jnp.concatenate([c[:, h:h + 1], jnp.zeros((rows_h - t_new, 1), F32)], axis=0), (rows_h, LANES))
            for j in range(t_new):
                kx_sc[j, rs, :] = jnp.broadcast_to(kn[j:j + 1, :], (rows_h, hd))
                vx_sc[j, rs, :] = jnp.broadcast_to(v[j:j + 1, cs], (rows_h, hd))
                ck_sc[j, rs, :] = jnp.broadcast_to(c[j:j + 1, h:h + 1], (rows_h, LANES))
        m_sc[...] = jnp.full(m_sc.shape, -jnp.inf, F32)
        l_sc[...] = jnp.zeros(l_sc.shape, F32)
        acc_sc[...] = jnp.zeros(acc_sc.shape, F32)
        carry_sc[...] = jnp.zeros(carry_sc.shape, F32)

    r = lax.broadcasted_iota(jnp.int32, (page, page), 0)
    c_ = lax.broadcasted_iota(jnp.int32, (page, page), 1)
    later = jnp.where(r > c_, 1.0, 0.0).astype(BF16)
    carry = carry_sc[...]
    for g in reversed(range(group)):
        ft = fp[g][0]
        suffix = _dot01_r(ft, later) + carry
        carry = carry + jnp.sum(ft, axis=-1, keepdims=True)
        for h in range(FOX_HEADS):
            rs = slice(h * rows_h, (h + 1) * rows_h)
            kh = kp[g][0, :, h, :].astype(BF16)
            s = _dot_nt(q_sc[rs, :].astype(BF16), kh)
            s_sc[rs, g * page:(g + 1) * page] = s + suffix[h:h + 1, :] + cq_sc[rs, 0:1]
    carry_sc[...] = carry

    s_all = s_sc[...]
    m_prev = m_sc[...]
    m_new = jnp.maximum(m_prev, jnp.max(s_all, axis=-1, keepdims=True))
    alpha = jnp.exp(m_prev - m_new)
    p_all = jnp.exp(s_all - m_new)
    l_sc[...] = alpha * l_sc[...] + jnp.sum(p_all, axis=-1, keepdims=True)
    m_sc[...] = m_new
    s_sc[...] = p_all
    for h in range(FOX_HEADS):
        rs = slice(h * rows_h, (h + 1) * rows_h)
        o_h = jnp.zeros((rows_h, hd), F32)
        for g in range(group):
            vh = vp[g][0, :, h, :].astype(BF16)
            o_h = o_h + _dot(s_sc[rs, g * page:(g + 1) * page].astype(BF16), vh)
        acc_sc[rs, :] = alpha[rs, :] * acc_sc[rs, :] + o_h

    @pl.when(step == n_groups - 1)
    def _():
        t_row = lax.broadcasted_iota(jnp.int32, (FOX_HEADS * rows_h, 1), 0) % rows_h
        q_all = q_sc[...]
        cq = cq_sc[:, 0:1]
        s_new = []
        for j in range(t_new):
            s_j = jnp.sum(q_all * kx_sc[j], axis=-1, keepdims=True) + cq - ck_sc[j, :, 0:1]
            s_new.append(jnp.where(t_row >= j, s_j, -jnp.inf))
        m_prev = m_sc[...]
        m_new = m_prev
        for s_j in s_new:
            m_new = jnp.maximum(m_new, s_j)
        alpha = jnp.exp(m_prev - m_new)
        l = alpha * l_sc[...]
        acc = alpha * acc_sc[...]
        for j, s_j in enumerate(s_new):
            p_j = jnp.exp(s_j - m_new)
            l = l + p_j
            acc = acc + p_j * vx_sc[j]
        o = acc / l
        for h in range(FOX_HEADS):
            o_ref[0, :, h * hd:(h + 1) * hd] = o[h * rows_h:h * rows_h + t_new, :]


def _fox_sample_attention(q, k, v, f, cache_k, cache_v, cache_ft, page_table, gq, gk):
    batch, t_new, width = q.shape
    n_pages = page_table.shape[1]
    page = cache_k.shape[1]
    group = FOX_PAGE_GROUP
    n_groups = n_pages // group
    rows = FOX_HEADS * SUBLANES

    def page_spec(shape, g):
        def index(b, s, pt):
            return (pt[b * n_pages + (n_groups - 1 - s) * group + g],) + (0,) * (len(shape) - 1)
        return pl.BlockSpec((1,) + shape[1:], index)

    tok_spec = lambda last: pl.BlockSpec((1, t_new, last), lambda b, s, pt: (b, 0, 0))
    in_specs = [tok_spec(width), tok_spec(width), tok_spec(width), tok_spec(FOX_HEADS),
                pl.BlockSpec((1, FOX_HEAD_DIM), lambda b, s, pt: (0, 0)),
                pl.BlockSpec((1, FOX_HEAD_DIM), lambda b, s, pt: (0, 0))]
    in_specs += [page_spec(cache_k.shape, g) for g in range(group)]
    in_specs += [page_spec(cache_v.shape, g) for g in range(group)]
    in_specs += [page_spec(cache_ft.shape, g) for g in range(group)]
    grid_spec = pltpu.PrefetchScalarGridSpec(
        num_scalar_prefetch=1,
        grid=(batch, n_groups),
        in_specs=in_specs,
        out_specs=[tok_spec(width), tok_spec(width)],
        scratch_shapes=[
            pltpu.VMEM((rows, FOX_HEAD_DIM), F32),
            pltpu.VMEM((t_new, rows, FOX_HEAD_DIM), F32),
            pltpu.VMEM((t_new, rows, FOX_HEAD_DIM), F32),
            pltpu.VMEM((rows, LANES), F32),
            pltpu.VMEM((t_new, rows, LANES), F32),
            pltpu.VMEM((rows, group * page), F32),
            pltpu.VMEM((rows, 1), F32),
            pltpu.VMEM((rows, 1), F32),
            pltpu.VMEM((rows, FOX_HEAD_DIM), F32),
            pltpu.VMEM((FOX_HEADS, 1), F32),
        ],
    )
    return pl.pallas_call(
        functools.partial(_fox_sample_kernel, group=group, n_groups=n_groups, t_new=t_new),
        grid_spec=grid_spec,
        out_shape=[jax.ShapeDtypeStruct(q.shape, F32), jax.ShapeDtypeStruct(q.shape, F32)],
        compiler_params=_params("parallel", "arbitrary"),
        name="fox_sample_attention",
    )(page_table.reshape(-1), q, k, v, f, gq.reshape(1, -1), gk.reshape(1, -1),
      *([cache_k] * group), *([cache_v] * group), *([cache_ft] * group))


def _fox_out_body(rows, consts, outs, is_sample, tile_index):
    (x_ref, o_ref, gate_ref), (w_ref,), (y_ref,) = rows, consts, outs
    gated = (o_ref[...] * jax.nn.sigmoid(gate_ref[...])).astype(BF16)
    y_ref[...] = x_ref[...] + _dot(gated, w_ref[...])


def _fox_layer(x, g, w_in, b_f, gq, gk, w_out, cache_k, cache_v, cache_f, page_table, dims):
    batch, seq, dec_batch, dec_seq = dims
    q, k, v, gate, f, ft = _fox_project(x, g, w_in, b_f)
    dcum, dcum_t = _fox_cumsum(f[0], ft[0], batch, seq)
    o_p, kn_p = _fox_prompt_attention(q[0], k[0], v[0], dcum, dcum_t, gq, gk, batch, seq)
    per_seq = lambda a: a.reshape(dec_batch, dec_seq, a.shape[-1])
    cache_ft = jnp.swapaxes(cache_f, 1, 2)
    o_s, kn_s = _fox_sample_attention(per_seq(q[1]), per_seq(k[1]), per_seq(v[1]), per_seq(f[1]),
                                      cache_k, cache_v, cache_ft, page_table, gq, gk)
    o = (o_p, o_s.reshape(-1, FOX_WIDTH))
    (y,) = _row_call(_fox_out_body, [x, o, gate], [w_out.astype(BF16)], [(w_out.shape[1], F32, False)], "fox_out")
    heads = lambda a, b, t: a.reshape(b, t, FOX_HEADS, FOX_HEAD_DIM)
    state = (heads(kn_p, batch, seq), heads(v[0], batch, seq), f[0].reshape(batch, seq, FOX_HEADS),
             heads(kn_s, dec_batch, dec_seq), heads(v[1], dec_batch, dec_seq),
             f[1].reshape(dec_batch, dec_seq, FOX_HEADS))
    return y, state


RWKV_HEAD_DIM = 64
RWKV_GN_EPS = 64e-5
RWKV_GATE_LORA_PAD = 256
RWKV_CHUNK = 64
RWKV_CHUNKS_PER_STEP = 4
RWKV_SEQS_PER_STEP = 8


def _rwkv_proj_body(rows, consts, outs, is_sample, tile_index, *, seq, dec_seq):
    x_ref, prev_ref, start_ref = rows
    (g_ref, mu_ref, w_rkv_ref, w0_ref, w1_ref, w2_ref, a0_ref, a1_ref, a2_ref, g1_ref, g2_ref) = consts
    r_ref, k_ref, v_ref, ld_ref, a_ref, gate_ref, xn_ref, last_ref = outs
    tile = x_ref.shape[0]
    xn = _rmsnorm(x_ref[...], g_ref[...])
    rolled = pltpu.roll(xn, 1, 0)
    row = lax.broadcasted_iota(jnp.int32, (tile, 1), 0)
    if is_sample:
        xn_ref[...] = xn
        x_prev = jnp.where(row % dec_seq == 0, start_ref[...], rolled)
    else:
        before = _rmsnorm(prev_ref[SUBLANES - 1:SUBLANES, :], g_ref[...])
        before = jnp.where((tile_index * tile) % seq == 0, 0.0, before)
        x_prev = jnp.where(row == 0, before, rolled)

        @pl.when(((tile_index + 1) * tile) % seq == 0)
        def _():
            last_ref[pl.ds((tile_index * tile) // seq, 1), :] = xn[tile - 1:tile, :]

    xx = x_prev - xn
    mix = lambda i: (xn + xx * mu_ref[i:i + 1, :]).astype(BF16)
    r_ref[...] = _dot(mix(0), w_rkv_ref[0])
    k_ref[...] = _dot(mix(2), w_rkv_ref[1])
    v_ref[...] = _dot(mix(3), w_rkv_ref[2])
    wl = w0_ref[...] + _dot(jnp.tanh(_dot(mix(1), w1_ref[...])).astype(BF16), w2_ref[...])
    softplus_neg = jnp.maximum(-wl, 0.0) + jnp.log1p(jnp.exp(-jnp.abs(wl)))
    ld_ref[...] = -jnp.exp(-softplus_neg - 0.5)
    a_ref[...] = jax.nn.sigmoid(a0_ref[...] + _dot(_dot(mix(4), a1_ref[...]).astype(BF16), a2_ref[...]))
    gate_ref[...] = _dot(jax.nn.sigmoid(_dot(mix(5), g1_ref[...])).astype(BF16), g2_ref[...])


def _dot3(a, b):
    a_hi = a.astype(BF16)
    a_lo = (a - a_hi.astype(F32)).astype(BF16)
    b_hi = b.astype(BF16)
    b_lo = (b - b_hi.astype(F32)).astype(BF16)
    return _dot(a_hi, b_hi) + _dot(a_hi, b_lo) + _dot(a_lo, b_hi)


def _pair_sums(x, first_head):
    s0 = jnp.sum(jnp.where(first_head, x, 0.0), axis=-1, keepdims=True)
    s1 = jnp.sum(jnp.where(first_head, 0.0, x), axis=-1, keepdims=True)
    return jnp.where(first_head, s0, s1)


def _rwkv_chunk_terms(r, k, v, ld, a, kk_w, ka_w):
    length = r.shape[0]
    first_head = lax.broadcasted_iota(jnp.int32, (1, LANES), 1) < RWKV_HEAD_DIM
    kk = k * kk_w
    kkn = kk / jnp.maximum(jnp.sqrt(_pair_sums(kk * kk, first_head)), 1e-12)
    k2 = k * (1.0 + (a - 1.0) * ka_w)
    aa, bb = -kkn, kkn * a
    ti = lax.broadcasted_iota(jnp.int32, (length, length), 0)
    si = lax.broadcasted_iota(jnp.int32, (length, length), 1)
    strict, incl = si < ti, si <= ti
    cl = _dot01(jnp.where(incl, 1.0, 0.0).astype(BF16), ld)
    cl_last = cl[length - 1:length, :]
    p_in, p_out = jnp.exp(cl), jnp.exp(-cl)
    r_t, a_t = r * p_in, aa * jnp.exp(cl - ld)
    b_t, k_t = bb * p_out, k2 * p_out
    p_end = jnp.exp(cl_last - cl)
    b_end, k_end = (bb * p_end).astype(BF16), (k2 * p_end).astype(BF16)
    rhs = jnp.concatenate([b_t, k_t], axis=0).astype(BF16)
    vb = v.astype(BF16)
    levels = max(1, math.ceil(math.log2(length)))
    w_sum = jnp.zeros((length, LANES), F32)
    yr_sum = jnp.zeros((length, LANES), F32)
    u0_heads, y0_heads = [], []
    for e in (0, 1):
        mine = first_head if e == 0 else jnp.logical_not(first_head)
        a_e, r_e = jnp.where(mine, a_t, 0.0), jnp.where(mine, r_t, 0.0)
        mm = _dot_nt(jnp.concatenate([a_e, r_e], axis=0).astype(BF16), rhs)
        m_ab = jnp.where(strict, mm[:length, :length], 0.0)
        m_ak = jnp.where(strict, mm[:length, length:], 0.0)
        m_rb = jnp.where(incl, mm[length:, :length], 0.0).astype(BF16)
        m_rk = jnp.where(incl, mm[length:, length:], 0.0).astype(BF16)
        x = jnp.concatenate([a_e, _dot(m_ak.astype(BF16), vb)], axis=1)
        power = m_ab
        for lvl in range(levels):
            x = x + _dot3(power, x)
            if lvl + 1 < levels:
                power = _dot3(power, power)
        w_e, u0_e = x[:, :LANES], x[:, LANES:]
        w_sum = w_sum + w_e
        yr_sum = yr_sum + r_e + _dot(m_rb, w_e.astype(BF16))
        u0_heads.append(u0_e)
        y0_heads.append(_dot(m_rb, u0_e.astype(BF16)) + _dot(m_rk, vb))
    u0 = jnp.where(first_head, u0_heads[0], u0_heads[1])
    y0 = jnp.where(first_head, y0_heads[0], y0_heads[1])
    row = lax.broadcasted_iota(jnp.int32, (LANES, LANES), 0)
    col = lax.broadcasted_iota(jnp.int32, (LANES, LANES), 1)
    same_head = (row < RWKV_HEAD_DIM) == (col < RWKV_HEAD_DIM)
    decay_end = jnp.where(row == col, jnp.exp(cl_last), 0.0)
    g = jnp.where(same_head, decay_end + _dot_tn(b_end, w_sum.astype(BF16)), 0.0)
    h = jnp.where(same_head, _dot_tn(b_end, u0.astype(BF16)) + _dot_tn(k_end, vb), 0.0)
    return g, h, yr_sum, y0, k2


def _rwkv_finish(y, r, k2, v, gate, rk_w, gnw, gnb):
    first_head = lax.broadcasted_iota(jnp.int32, (1, LANES), 1) < RWKV_HEAD_DIM
    mean = _pair_sums(y, first_head) / RWKV_HEAD_DIM
    yc = y - mean
    var = _pair_sums(yc * yc, first_head) / RWKV_HEAD_DIM
    yn = yc * lax.rsqrt(var + RWKV_GN_EPS) * gnw + gnb
    bonus = _pair_sums(r * k2 * rk_w, first_head) * v
    return (yn + bonus) * gate


def _pair_state_in(s_ref):
    z = jnp.zeros((RWKV_HEAD_DIM, RWKV_HEAD_DIM), F32)
    top = jnp.concatenate([s_ref[0].T, z], axis=1)
    bot = jnp.concatenate([z, s_ref[1].T], axis=1)
    return jnp.concatenate([top, bot], axis=0)


def _pair_state_out(s_ref, a):
    at = a.T
    s_ref[0] = at[:RWKV_HEAD_DIM, :RWKV_HEAD_DIM]
    s_ref[1] = at[RWKV_HEAD_DIM:, RWKV_HEAD_DIM:]


def _rwkv_scan_kernel(r_ref, k_ref, v_ref, ld_ref, a_ref, gate_ref, kk_ref, ka_ref, rk_ref, gnw_ref, gnb_ref,
                      s0_ref, z_ref, s_ref, state_sc, *, length, n_inst, chained):
    step = pl.program_id(2)
    if chained:
        @pl.when(step == 0)
        def _():
            state_sc[...] = _pair_state_in(s0_ref.at[0])
        state = state_sc[...]
    for i in range(n_inst):
        rs = slice(i * length, (i + 1) * length)
        r, k, v, ld, a = r_ref[rs, :], k_ref[rs, :], v_ref[rs, :], ld_ref[rs, :], a_ref[rs, :]
        g, h, yr, y0, k2 = _rwkv_chunk_terms(r, k, v, ld, a, kk_ref[...], ka_ref[...])
        if not chained:
            state = _pair_state_in(s0_ref.at[i])
        y = _dot3(yr, state) + y0
        state = _dot3(g, state) + h
        z_ref[rs, :] = _rwkv_finish(y, r, k2, v, gate_ref[rs, :], rk_ref[...], gnw_ref[...], gnb_ref[...])
        if not chained:
            _pair_state_out(s_ref.at[i], state)
    if chained:
        state_sc[...] = state

        @pl.when(step == pl.num_programs(2) - 1)
        def _():
            _pair_state_out(s_ref.at[0], state)


def _rwkv_scan(r, k, v, ld, a, gate, s0, params, n_seq, seq, length, n_inst, chained):
    d = r.shape[1]
    pairs = d // LANES
    heads = d // RWKV_HEAD_DIM
    rows = n_inst * length
    if chained:
        steps = seq // rows
        grid = (n_seq, pairs, steps)
        row_map = lambda b, p, s: (b * steps + s, p)
        st_block, st_map = (1, 2, RWKV_HEAD_DIM, RWKV_HEAD_DIM), (lambda b, p, s: (b, p, 0, 0))
    else:
        assert seq == length
        grid = (n_seq // n_inst, pairs, 1)
        row_map = lambda b, p, s: (b, p)
        st_block, st_map = (n_inst, 2, RWKV_HEAD_DIM, RWKV_HEAD_DIM), (lambda b, p, s: (b, p, 0, 0))
    row_spec = pl.BlockSpec((rows, LANES), row_map)
    par_spec = pl.BlockSpec((1, LANES), lambda b, p, s: (0, p))
    return pl.pallas_call(
        functools.partial(_rwkv_scan_kernel, length=length, n_inst=n_inst, chained=chained),
        grid=grid,
        in_specs=[row_spec] * 6 + [par_spec] * 5 + [pl.BlockSpec(st_block, st_map)],
        out_specs=[row_spec, pl.BlockSpec(st_block, st_map)],
        out_shape=[jax.ShapeDtypeStruct(r.shape, F32), jax.ShapeDtypeStruct((n_seq, heads, RWKV_HEAD_DIM, RWKV_HEAD_DIM), F32)],
        scratch_shapes=[pltpu.VMEM((LANES, LANES), F32)],
        compiler_params=_params("parallel", "parallel", "arbitrary"),
        name="rwkv_scan_prompt" if chained else "rwkv_scan_sample",
    )(r, k, v, ld, a, gate, *params, s0)


def _rwkv_out_body(rows, consts, outs, is_sample, tile_index):
    (x_ref, z_ref), (w_ref,), (y_ref,) = rows, consts, outs
    y_ref[...] = x_ref[...] + _dot(z_ref[...].astype(BF16), w_ref[...])


def _rwkv_layer(x, g, shift_state, wkv_state, mu, w0, w1, w2, a0, a1, a2, g1, g2, kk_w, ka_w, rk_w, w_rkv, w_o,
                gn_w, gn_b, dims):
    batch, seq, dec_batch, dec_seq = dims
    d = mu.shape[1]
    heads = d // RWKV_HEAD_DIM
    row = lambda t: t.reshape(1, d)
    starts = jnp.concatenate([shift_state[:, None, :], jnp.zeros((dec_batch, dec_seq - 1, d), F32)], axis=1)
    rank = g1.shape[1]
    g1p = jnp.pad(g1, ((0, 0), (0, RWKV_GATE_LORA_PAD - rank))).astype(BF16)
    g2p = jnp.pad(g2, ((0, RWKV_GATE_LORA_PAD - rank), (0, 0))).astype(BF16)
    consts = [row(g), mu, w_rkv.astype(BF16), row(w0), w1.astype(BF16), w2.astype(BF16), row(a0), a1.astype(BF16),
              a2.astype(BF16), g1p, g2p]
    outs = [(d, F32, False)] * 6 + [(d, F32, "sample"), ((batch, d), F32, "whole")]
    r, k, v, ld, a, gate, xn, last_p = _row_call(
        functools.partial(_rwkv_proj_body, seq=seq, dec_seq=dec_seq),
        [x, (x[0], None, "prev"), (None, starts.reshape(-1, d))], consts, outs, "rwkv_proj", tile_div=2)
    params = [row(t) for t in (kk_w, ka_w, rk_w.reshape(-1), gn_w, gn_b)]
    zero_state = jnp.zeros((batch, heads, RWKV_HEAD_DIM, RWKV_HEAD_DIM), F32)
    z_p, s_p = _rwkv_scan(r[0], k[0], v[0], ld[0], a[0], gate[0], zero_state, params, batch, seq,
                          RWKV_CHUNK, RWKV_CHUNKS_PER_STEP, True)
    padded = lambda t: jnp.pad(t.reshape(dec_batch, dec_seq, d), ((0, 0), (0, SUBLANES - dec_seq), (0, 0))).reshape(-1, d)
    z_s, s_s = _rwkv_scan(*[padded(t[1]) for t in (r, k, v, ld, a, gate)], wkv_state, params, dec_batch, SUBLANES,
                          SUBLANES, RWKV_SEQS_PER_STEP, False)
    z_s = z_s.reshape(dec_batch, SUBLANES, d)[:, :dec_seq].reshape(-1, d)
    (y,) = _row_call(_rwkv_out_body, [x, (z_p, z_s)], [w_o.astype(BF16)], [(d, F32, False)], "rwkv_out")
    shift_s = xn[1].reshape(dec_batch, dec_seq, d)[:, dec_seq - 1]
    return y, (last_p, s_p, shift_s, s_s)


MLSTM_HEADS = 4
MLSTM_QK_DIM = 128
MLSTM_V_DIM = 256
MLSTM_CHUNK = 64
MLSTM_GATE_CAP = 15.0
MLSTM_CHUNKS_PER_STEP = 4
MLSTM_SEQS_PER_STEP = 16
MLSTM_NO_INPUT = -1e30


def _mlstm_proj_body(rows, consts, outs, is_sample, tile_index):
    (x_ref,), (g_ref, w_ref, wg_ref, bg_ref) = rows, consts
    q_ref, k_ref, v_ref, og_ref, ig_ref, igt_ref = outs
    hk = MLSTM_HEADS * MLSTM_QK_DIM
    hv = MLSTM_HEADS * MLSTM_V_DIM
    xn = _rmsnorm(x_ref[...], g_ref[...]).astype(BF16)
    q_ref[...] = _dot(xn, w_ref[:, :hk]) * MLSTM_QK_DIM ** -0.5
    k_ref[...] = _dot(xn, w_ref[:, hk:2 * hk])
    v_ref[...] = _dot(xn, w_ref[:, 2 * hk:2 * hk + hv])
    og_ref[...] = _dot(xn, w_ref[:, 2 * hk + hv:])
    gates = MLSTM_GATE_CAP * jnp.tanh((_dot(xn, wg_ref[...]) + bg_ref[...]) / MLSTM_GATE_CAP)
    is_forget = lax.broadcasted_iota(jnp.int32, (1, LANES), 1) >= MLSTM_HEADS
    ig = jnp.where(is_forget, _log_sigmoid(gates), gates)
    ig_ref[...] = ig[:, :2 * MLSTM_HEADS]
    igt_ref[...] = ig.T[:2 * MLSTM_HEADS, :]


def _mlstm_chunk(q, k, v, og, ig, igt, head, c, n, m, gn):
    length = q.shape[0]
    nh = MLSTM_HEADS
    ti = lax.broadcasted_iota(jnp.int32, (length, length), 0)
    si = lax.broadcasted_iota(jnp.int32, (length, length), 1)
    causal = si <= ti
    lane_g = lax.broadcasted_iota(jnp.int32, (1, 2 * nh), 1)
    sub_g = lax.broadcasted_iota(jnp.int32, (2 * nh, 1), 0)
    col = lambda x, j: jnp.sum(jnp.where(lane_g == j, x, 0.0), axis=-1, keepdims=True)
    rowv = lambda x, j: jnp.sum(jnp.where(sub_g == j, x, 0.0), axis=0, keepdims=True)
    cum_c = _dot01(jnp.where(causal, 1.0, 0.0).astype(BF16), ig)
    cum_r = _dot01_r(igt, jnp.where(ti <= si, 1.0, 0.0).astype(BF16))
    b_c, li_c = col(cum_c, nh + head), col(ig, head)
    b_r, li_r = rowv(cum_r, nh + head), rowv(igt, head)
    logw = jnp.where(causal, b_c - b_r + li_r, -jnp.inf)
    inter = m + b_c
    m_t = jnp.maximum(inter, jnp.max(logw, axis=-1, keepdims=True))
    wts = jnp.exp(logw - m_t)
    inter_scale = jnp.exp(inter - m_t)
    qb, kb, vb = q.astype(BF16), k.astype(BF16), v.astype(BF16)
    sw = wts * _dot_nt(qb, kb)
    num = inter_scale * _dot_nt(qb, c.astype(BF16)) + _dot(sw.astype(BF16), vb)
    den = inter_scale * jnp.sum(q * n, axis=-1, keepdims=True) + jnp.sum(sw, axis=-1, keepdims=True)
    h = num / jnp.maximum(jnp.abs(den), jnp.exp(-m_t))
    m_new = m_t[length - 1:length, :]
    carry_scale = jnp.exp(inter[length - 1:length, :] - m_new)
    w_s = jnp.exp(b_c[length - 1:length, :] - b_c + li_c - m_new)
    c_new = carry_scale * c + _dot_tn((v * w_s).astype(BF16), kb)
    n_new = carry_scale * n + jnp.sum(w_s * k, axis=0, keepdims=True)
    h = h * lax.rsqrt(jnp.mean(h * h, axis=-1, keepdims=True) + NORM_EPS)
    return h * gn * jax.nn.sigmoid(og), c_new, n_new, m_new


def _mlstm_scan_kernel(q_ref, k_ref, v_ref, og_ref, ig_ref, igt_ref, gn_ref, c0_ref, nm0_ref,
                       h_ref, c_ref, nm_ref, c_sc, nm_sc, *, length, n_inst, chained):
    head = pl.program_id(1)
    step = pl.program_id(2)
    if chained:
        @pl.when(step == 0)
        def _():
            c_sc[...] = c0_ref[0, 0]
            nm_sc[...] = nm0_ref[0, 0]
        c, n, m = c_sc[...], nm_sc[0:1, :], nm_sc[1:2, 0:1]
    for i in range(n_inst):
        rs = slice(i * length, (i + 1) * length)
        if not chained:
            c, n, m = c0_ref[i, 0], nm0_ref[i, 0, 0:1, :], nm0_ref[i, 0, 1:2, 0:1]
        h, c, n, m = _mlstm_chunk(q_ref[rs, :], k_ref[rs, :], v_ref[rs, :], og_ref[rs, :], ig_ref[rs, :],
                                  igt_ref[:, rs], head, c, n, m, gn_ref[...])
        h_ref[rs, :] = h
        if not chained:
            c_ref[i, 0] = c
            nm_ref[i, 0] = jnp.concatenate([n, jnp.broadcast_to(m, n.shape)], axis=0)
    if chained:
        c_sc[...] = c
        nm = jnp.concatenate([n, jnp.broadcast_to(m, n.shape)], axis=0)
        nm_sc[...] = nm

        @pl.when(step == pl.num_programs(2) - 1)
        def _():
            c_ref[0, 0] = c
            nm_ref[0, 0] = nm


def _mlstm_scan(q, k, v, og, ig, igt, gn, c0, nm0, n_seq, seq, length, n_inst, chained):
    dk, dv, nh = MLSTM_QK_DIM, MLSTM_V_DIM, MLSTM_HEADS
    rows = n_inst * length
    if chained:
        steps = seq // rows
        grid = (n_seq, nh, steps)
        r_idx = lambda b, h, s: b * steps + s
        sb = 1
    else:
        assert seq == length
        grid = (n_seq // n_inst, nh, 1)
        r_idx = lambda b, h, s: b
        sb = n_inst
    c_spec = pl.BlockSpec((sb, 1, dv, dk), lambda b, h, s: (b, h, 0, 0))
    nm_spec = pl.BlockSpec((sb, 1, 2, dk), lambda b, h, s: (b, h, 0, 0))
    return pl.pallas_call(
        functools.partial(_mlstm_scan_kernel, length=length, n_inst=n_inst, chained=chained),
        grid=grid,
        in_specs=[
            pl.BlockSpec((rows, dk), lambda b, h, s: (r_idx(b, h, s), h)),
            pl.BlockSpec((rows, dk), lambda b, h, s: (r_idx(b, h, s), h)),
            pl.BlockSpec((rows, dv), lambda b, h, s: (r_idx(b, h, s), h)),
            pl.BlockSpec((rows, dv), lambda b, h, s: (r_idx(b, h, s), h)),
            pl.BlockSpec((rows, 2 * nh), lambda b, h, s: (r_idx(b, h, s), 0)),
            pl.BlockSpec((2 * nh, rows), lambda b, h, s: (0, r_idx(b, h, s))),
            pl.BlockSpec((1, dv), lambda b, h, s: (0, h)),
            c_spec, nm_spec,
        ],
        out_specs=[pl.BlockSpec((rows, dv), lambda b, h, s: (r_idx(b, h, s), h)), c_spec, nm_spec],
        out_shape=[jax.ShapeDtypeStruct(v.shape, F32), jax.ShapeDtypeStruct((n_seq, nh, dv, dk), F32),
                   jax.ShapeDtypeStruct((n_seq, nh, 2, dk), F32)],
        scratch_shapes=[pltpu.VMEM((dv, dk), F32), pltpu.VMEM((2, dk), F32)],
        compiler_params=_params("parallel", "parallel", "arbitrary"),
        name="mlstm_scan_prompt" if chained else "mlstm_scan_sample",
    )(q, k, v, og, ig, igt, gn, c0, nm0)


def _mlstm_out_body(rows, consts, outs, is_sample, tile_index):
    (x_ref, h_ref), (w_ref,), (y_ref,) = rows, consts, outs
    y_ref[...] = x_ref[...] + _dot(h_ref[...].astype(BF16), w_ref[...])


def _mlstm_layer(x, g, c_state, n_state, m_state, w_in, b_if, gn, w_out, dims):
    batch, seq, dec_batch, dec_seq = dims
    dk, dv, nh = MLSTM_QK_DIM, MLSTM_V_DIM, MLSTM_HEADS
    d = w_in.shape[0]
    n_main = 2 * nh * dk + 2 * nh * dv
    w_main = w_in[:, :n_main].astype(BF16)
    w_g = jnp.pad(w_in[:, n_main:], ((0, 0), (0, LANES - 2 * nh))).astype(BF16)
    b_g = jnp.pad(b_if, (0, LANES - 2 * nh)).reshape(1, LANES)
    outs = [(nh * dk, F32, False), (nh * dk, F32, False), (nh * dv, F32, False), (nh * dv, F32, False),
            (2 * nh, F32, False), (2 * nh, F32, True)]
    q, k, v, og, ig, igt = _row_call(_mlstm_proj_body, [x], [g.reshape(1, d), w_main, w_g, b_g], outs, "mlstm_proj")
    gn2 = gn.reshape(1, nh * dv)
    zeros_c = jnp.zeros((batch, nh, dv, dk), F32)
    zeros_nm = jnp.zeros((batch, nh, 2, dk), F32)
    h_p, c_p, nm_p = _mlstm_scan(q[0], k[0], v[0], og[0], ig[0], igt[0], gn2, zeros_c, zeros_nm, batch, seq,
                                 MLSTM_CHUNK, MLSTM_CHUNKS_PER_STEP, True)
    extra = SUBLANES - dec_seq
    pad_rows = lambda t: jnp.pad(t.reshape(dec_batch, dec_seq, -1), ((0, 0), (0, extra), (0, 0))).reshape(dec_batch * SUBLANES, -1)
    ig_s = ig[1].reshape(dec_batch, dec_seq, 2 * nh)
    pad_gate = jnp.concatenate([jnp.full((dec_batch, extra, nh), MLSTM_NO_INPUT, F32), jnp.zeros((dec_batch, extra, nh), F32)], axis=-1)
    ig_s = jnp.concatenate([ig_s, pad_gate], axis=1).reshape(dec_batch * SUBLANES, 2 * nh)
    nm_s = jnp.stack([n_state, jnp.broadcast_to(m_state[..., None], n_state.shape)], axis=2)
    h_s, c_s, nm_s = _mlstm_scan(pad_rows(q[1]), pad_rows(k[1]), pad_rows(v[1]), pad_rows(og[1]), ig_s, ig_s.T, gn2,
                                 c_state, nm_s, dec_batch, SUBLANES, SUBLANES, MLSTM_SEQS_PER_STEP, False)
    h_s = h_s.reshape(dec_batch, SUBLANES, nh * dv)[:, :dec_seq].reshape(-1, nh * dv)
    (y,) = _row_call(_mlstm_out_body, [x, (h_p, h_s)], [w_out.astype(BF16)], [(d, F32, False)], "mlstm_out")
    state = (c_p, nm_p[:, :, 0], nm_p[:, :, 1, 0], c_s, nm_s[:, :, 0], nm_s[:, :, 1, 0])
    return y, state


def _final_norm_body(rows, consts, outs, is_sample, tile_index):
    outs[0][...] = _rmsnorm(rows[0][...], consts[0][...])


def _final_norm(x, g):
    d = g.shape[0]
    return _row_call(_final_norm_body, [x], [g.reshape(1, d)], [(d, F32, False)], "final_norm")[0]


N_MIXERS = 3


def kernel(x_prompt, x_sample,
           cache_k_l0, cache_v_l0, cache_f_l0,
           state_shift_l1, state_wkv_l1,
           state_c_l2, state_n_l2, state_m_l2,
           cache_k_l3, cache_v_l3, cache_f_l3,
           page_table,
           norm_g, final_g, ffn_w_in, ffn_w_out,
           fox_w_in, fox_b_f, fox_gq, fox_gk, fox_w_out,
           rw_mu, rw_w0, rw_w1, rw_w2, rw_a0, rw_a1, rw_a2, rw_g1, rw_g2,
           rw_kk, rw_ka, rw_rk, rw_w_rkv, rw_w_o, rw_gn_w, rw_gn_b,
           ml_w_in, ml_b_if, ml_gn, ml_w_out):
    batch, seq, d = x_prompt.shape
    dec_batch, dec_seq, _ = x_sample.shape
    dims = (batch, seq, dec_batch, dec_seq)
    depth = norm_g.shape[0]
    x = (x_prompt.reshape(batch * seq, d), x_sample.reshape(dec_batch * dec_seq, d))
    fox_caches = ((cache_k_l0, cache_v_l0, cache_f_l0), (cache_k_l3, cache_v_l3, cache_f_l3))
    rwkv_states = ((state_shift_l1, state_wkv_l1),)
    mlstm_states = ((state_c_l2, state_n_l2, state_m_l2),)
    w_in_bf, w_out_bf = ffn_w_in.astype(BF16), ffn_w_out.astype(BF16)
    layer_state = []
    for i in range(depth):
        kind, j = i % N_MIXERS, i // N_MIXERS
        x = _ffn(x, norm_g[i, 0], w_in_bf[i, 0], w_out_bf[i, 0])
        if kind == 0:
            ck, cv, cf = fox_caches[j]
            x, st = _fox_layer(x, norm_g[i, 1], fox_w_in[j], fox_b_f[j], fox_gq[j], fox_gk[j], fox_w_out[j],
                               ck, cv, cf, page_table, dims)
        elif kind == 1:
            sh_in, wkv_in = rwkv_states[j]
            x, st = _rwkv_layer(x, norm_g[i, 1], sh_in, wkv_in, rw_mu[j], rw_w0[j], rw_w1[j], rw_w2[j], rw_a0[j],
                                rw_a1[j], rw_a2[j], rw_g1[j], rw_g2[j], rw_kk[j], rw_ka[j], rw_rk[j], rw_w_rkv[j],
                                rw_w_o[j], rw_gn_w[j], rw_gn_b[j], dims)
        else:
            c_in, n_in, m_in = mlstm_states[j]
            x, st = _mlstm_layer(x, norm_g[i, 1], c_in, n_in, m_in, ml_w_in[j], ml_b_if[j], ml_gn[j], ml_w_out[j], dims)
        layer_state.append(st)
        x = _ffn(x, norm_g[i, 2], w_in_bf[i, 1], w_out_bf[i, 1])
    y_p, y_s = _final_norm(x, final_g)
    out = (y_p.reshape(batch, seq, d), y_s.reshape(dec_batch, dec_seq, d))
    for st in layer_state:
        out = out + tuple(st)
    return out
```

```python
import functools
import math

import jax
import jax.numpy as jnp
from jax import lax
from jax.experimental import pallas as pl
from jax.experimental.pallas import tpu as pltpu

F32 = jnp.float32
BF16 = jnp.bfloat16

NORM_EPS = 1e-6
VMEM_LIMIT_BYTES = 56 * 1024 * 1024
ROW_TILE = 512
LANES = 128
SUBLANES = 8


def _params(*semantics):
    return pltpu.CompilerParams(dimension_semantics=semantics, vmem_limit_bytes=VMEM_LIMIT_BYTES)


def _const_spec(shape):
    return pl.BlockSpec(shape, lambda *_: (0,) * len(shape))


def _rmsnorm(x, g):
    ms = jnp.mean(x * x, axis=-1, keepdims=True)
    return x * lax.rsqrt(ms + NORM_EPS) * g


def _dot(a, b):
    return jnp.dot(a, b, preferred_element_type=F32)


def _dot_nt(a, b):
    return lax.dot_general(a, b, (((1,), (1,)), ((), ())), preferred_element_type=F32)


def _dot_tn(a, b):
    return lax.dot_general(a, b, (((0,), (0,)), ((), ())), preferred_element_type=F32)


FFN_CHUNK = 256


def _ffn_body(rows, consts, outs, is_sample, tile_index):
    (x_ref,), (g_ref, win_ref, wout_ref), (o_ref,) = rows, consts, outs
    d_ff = wout_ref.shape[0]
    x = x_ref[...]
    xn = _rmsnorm(x, g_ref[...]).astype(BF16)
    acc = jnp.zeros_like(x)
    for c in range(d_ff // FFN_CHUNK):
        lo = c * FFN_CHUNK
        gate = _dot(xn, win_ref[:, lo:lo + FFN_CHUNK])
        up = _dot(xn, win_ref[:, d_ff + lo:d_ff + lo + FFN_CHUNK])
        act = (gate * jax.nn.sigmoid(gate) * up).astype(BF16)
        acc = acc + _dot(act, wout_ref[lo:lo + FFN_CHUNK, :])
    o_ref[...] = x + 0.5 * acc


def _ffn(x, g, w_in, w_out):
    d = w_in.shape[0]
    return _row_call(_ffn_body, [x], [g.reshape(1, d), w_in, w_out], [(d, F32, False)], "ffn")[0]


def _row_call(body, rows, consts, outs, name, tile_div=1):
    n_p, n_s = rows[0][0].shape[0], rows[0][1].shape[0]
    tile = ROW_TILE // tile_div
    assert n_p % tile == 0 and n_s % tile == 0
    tp, ts = n_p // tile, n_s // tile
    p_idx = lambda i: jnp.minimum(i, tp - 1)
    s_idx = lambda i: jnp.maximum(i - tp, 0)
    per_tile = tile // SUBLANES
    prev_of = lambda t: jnp.maximum(t * per_tile - 1, 0)
    in_specs, args, row_slots = [], [], []
    for entry in rows:
        prev = len(entry) == 3
        slots = []
        for a, idx in ((entry[0], p_idx), (entry[1], s_idx)):
            if a is None:
                slots.append(None)
                continue
            if prev:
                in_specs.append(pl.BlockSpec((SUBLANES, a.shape[1]), lambda i, idx=idx: (prev_of(idx(i)), 0)))
            else:
                in_specs.append(pl.BlockSpec((tile, a.shape[1]), lambda i, idx=idx: (idx(i), 0)))
            slots.append(len(args))
            args.append(a)
        row_slots.append(slots)
    n_row_args = len(args)
    for c in consts:
        in_specs.append(_const_spec(c.shape))
        args.append(c)
    out_specs, out_shape, out_slots = [], [], []
    for cols, dtype, kind in outs:
        slots = []
        if kind == "whole":
            out_specs.append(_const_spec(cols))
            out_shape.append(jax.ShapeDtypeStruct(cols, dtype))
            slots = [len(out_shape) - 1] * 2
        else:
            for n, idx, absent in ((n_p, p_idx, kind == "sample"), (n_s, s_idx, False)):
                if absent:
                    slots.append(None)
                    continue
                if kind is True:
                    out_specs.append(pl.BlockSpec((cols, tile), lambda i, idx=idx: (0, idx(i))))
                    out_shape.append(jax.ShapeDtypeStruct((cols, n), dtype))
                else:
                    out_specs.append(pl.BlockSpec((tile, cols), lambda i, idx=idx: (idx(i), 0)))
                    out_shape.append(jax.ShapeDtypeStruct((n, cols), dtype))
                slots.append(len(out_shape) - 1)
        out_slots.append(slots)
    nc = len(consts)

    def kern(*refs):
        i = pl.program_id(0)
        r, c, o = refs[:n_row_args], refs[n_row_args:n_row_args + nc], refs[n_row_args + nc:]
        pick = lambda seq, slots, side: [None if s[side] is None else seq[s[side]] for s in slots]

        @pl.when(i < tp)
        def _():
            body(pick(r, row_slots, 0), c, pick(o, out_slots, 0), False, i)

        @pl.when(i >= tp)
        def _():
            body(pick(r, row_slots, 1), c, pick(o, out_slots, 1), True, i - tp)

    res = pl.pallas_call(
        kern, grid=(tp + ts,), in_specs=in_specs, out_specs=out_specs, out_shape=out_shape,
        compiler_params=_params("arbitrary"), name=name,
    )(*args)
    result = []
    for (cols, dtype, kind), slots in zip(outs, out_slots):
        if kind == "whole":
            result.append(res[slots[0]])
        else:
            result.append(tuple(None if s is None else res[s] for s in slots))
    return result


def _split3(x):
    hi = x.astype(BF16)
    r1 = x - hi.astype(F32)
    mid = r1.astype(BF16)
    lo = (r1 - mid.astype(F32)).astype(BF16)
    return hi, mid, lo


def _dot01(m01, x):
    hi, mid, lo = _split3(x)
    return _dot(m01, hi) + _dot(m01, mid) + _dot(m01, lo)


def _dot01_r(x, m01):
    hi, mid, lo = _split3(x)
    return _dot(hi, m01) + _dot(mid, m01) + _dot(lo, m01)


def _log_sigmoid(x):
    return jnp.minimum(x, 0.0) - jnp.log1p(jnp.exp(-jnp.abs(x)))


FOX_HEADS = 16
FOX_HEAD_DIM = 64
FOX_WIDTH = FOX_HEADS * FOX_HEAD_DIM
FOX_SCALE = FOX_HEAD_DIM ** -0.5


def _fox_proj_body(rows, consts, outs, is_sample, tile_index):
    (x_ref,) = rows
    g_ref, w_ref, wf_ref, bf_ref = consts
    q_ref, k_ref, v_ref, gate_ref, f_ref, ft_ref = outs
    xn = _rmsnorm(x_ref[...], g_ref[...]).astype(BF16)
    for j, o_ref in enumerate((q_ref, k_ref, v_ref, gate_ref)):
        o_ref[...] = _dot(xn, w_ref[:, j * FOX_WIDTH:(j + 1) * FOX_WIDTH])
    logf = _log_sigmoid(_dot(xn, wf_ref[...]) + bf_ref[...])
    f_ref[...] = logf[:, :FOX_HEADS]
    ft_ref[...] = logf.T[:FOX_HEADS, :]


def _fox_project(x, g, w_in, b_f):
    w_main = w_in[:, :4 * FOX_WIDTH].astype(BF16)
    w_f = jnp.pad(w_in[:, 4 * FOX_WIDTH:], ((0, 0), (0, LANES - FOX_HEADS))).astype(BF16)
    b = jnp.pad(b_f, (0, LANES - FOX_HEADS)).reshape(1, LANES)
    outs = [(FOX_WIDTH, F32, False)] * 4 + [(FOX_HEADS, F32, False), (FOX_HEADS, F32, True)]
    return _row_call(_fox_proj_body, [x], [g.reshape(1, -1), w_main, w_f, b], outs, "fox_proj")


def _fox_cumsum_kernel(f_ref, ft_ref, d_ref, dt_ref, *, seq):
    r = lax.broadcasted_iota(jnp.int32, (LANES, LANES), 0)
    c = lax.broadcasted_iota(jnp.int32, (LANES, LANES), 1)
    lower = jnp.where(c <= r, 1.0, 0.0).astype(BF16)
    upper = jnp.where(r <= c, 1.0, 0.0).astype(BF16)
    carry_c = jnp.zeros((1, FOX_HEADS), F32)
    carry_r = jnp.zeros((FOX_HEADS, 1), F32)
    for blk in range(seq // LANES):
        sl = slice(blk * LANES, (blk + 1) * LANES)
        loc = _dot01(lower, f_ref[sl, :]) + carry_c
        d_ref[sl, :] = loc
        carry_c = loc[LANES - 1:LANES, :]
        loc_r = _dot01_r(ft_ref[:, sl], upper) + carry_r
        dt_ref[:, sl] = loc_r
        carry_r = loc_r[:, LANES - 1:LANES]


def _fox_cumsum(f, ft, batch, seq):
    return pl.pallas_call(
        functools.partial(_fox_cumsum_kernel, seq=seq),
        grid=(batch,),
        in_specs=[pl.BlockSpec((seq, FOX_HEADS), lambda b: (b, 0)),
                  pl.BlockSpec((FOX_HEADS, seq), lambda b: (0, b))],
        out_specs=[pl.BlockSpec((seq, FOX_HEADS), lambda b: (b, 0)),
                   pl.BlockSpec((FOX_HEADS, seq), lambda b: (0, b))],
        out_shape=[jax.ShapeDtypeStruct(f.shape, F32), jax.ShapeDtypeStruct(ft.shape, F32)],
        compiler_params=_params("parallel"),
        name="fox_cumsum",
    )(f, ft)


def _pair_head_norm(x, g, first_head):
    x2 = x * x
    s0 = jnp.sum(jnp.where(first_head, x2, 0.0), axis=-1, keepdims=True)
    s1 = jnp.sum(jnp.where(first_head, 0.0, x2), axis=-1, keepdims=True)
    inv = jnp.where(first_head, lax.rsqrt(s0 / FOX_HEAD_DIM + NORM_EPS), lax.rsqrt(s1 / FOX_HEAD_DIM + NORM_EPS))
    return x * inv * g


def _fox_prompt_kernel(q_ref, k_ref, v_ref, d_ref, dt_ref, gq_ref, gk_ref, o_ref, kn_ref, k_sc, v_sc, *, tq):
    pair = pl.program_id(1)
    qi = pl.program_id(2)
    first_head = lax.broadcasted_iota(jnp.int32, (1, LANES), 1) < FOX_HEAD_DIM

    @pl.when(qi == 0)
    def _():
        kn = _pair_head_norm(k_ref[...], gk_ref[...], first_head)
        kn_ref[...] = kn
        k_sc[...] = kn.astype(BF16)
        v_sc[...] = v_ref[...].astype(BF16)

    qn = _pair_head_norm(q_ref[...], gq_ref[...], first_head) * FOX_SCALE
    q_heads = (jnp.where(first_head, qn, 0.0).astype(BF16), jnp.where(first_head, 0.0, qn).astype(BF16))
    d_blk = d_ref[...]
    head_lane = lax.broadcasted_iota(jnp.int32, (1, FOX_HEADS), 1)
    d_q = [jnp.sum(jnp.where(head_lane == 2 * pair + e, d_blk, 0.0), axis=-1, keepdims=True) for e in (0, 1)]
    row = lax.broadcasted_iota(jnp.int32, (tq, tq), 0)
    col = lax.broadcasted_iota(jnp.int32, (tq, tq), 1)

    def kv_block(j, carry, diagonal):
        start = pl.multiple_of(j * tq, tq)
        kb = k_sc[pl.ds(start, tq), :]
        vb = v_sc[pl.ds(start, tq), :]
        new = []
        for e in (0, 1):
            m, l, acc = carry[e]
            d_k = dt_ref[pl.ds(2 * pair + e, 1), pl.ds(start, tq)]
            s = _dot_nt(q_heads[e], kb) + (d_q[e] - d_k)
            if diagonal:
                s = jnp.where(col <= row, s, -jnp.inf)
            m_new = jnp.maximum(m, jnp.max(s, axis=-1, keepdims=True))
            alpha = jnp.exp(m - m_new)
            p = jnp.exp(s - m_new)
            l = alpha * l + jnp.sum(p, axis=-1, keepdims=True)
            acc = alpha * acc + _dot(p.astype(BF16), vb)
            new.append((m_new, l, acc))
        return tuple(new)

    init = tuple((jnp.full((tq, 1), -jnp.inf, F32), jnp.zeros((tq, 1), F32), jnp.zeros((tq, LANES), F32))
                 for _ in (0, 1))
    carry = lax.fori_loop(0, qi, lambda j, c: kv_block(j, c, False), init)
    (_, l0, acc0), (_, l1, acc1) = kv_block(qi, carry, True)
    o_ref[...] = jnp.where(first_head, acc0 / l0, acc1 / l1)


FOX_Q_BLOCK = 512


def _fox_prompt_attention(q, k, v, dcum, dcum_t, gq, gk, batch, seq):
    tq = FOX_Q_BLOCK
    nq = seq // tq
    n = batch * seq
    pairs = FOX_WIDTH // LANES
    gq2 = jnp.tile(gq, 2).reshape(1, LANES)
    gk2 = jnp.tile(gk, 2).reshape(1, LANES)
    return pl.pallas_call(
        functools.partial(_fox_prompt_kernel, tq=tq),
        grid=(batch, pairs, nq),
        in_specs=[
            pl.BlockSpec((tq, LANES), lambda b, p, i: (b * nq + i, p)),
            pl.BlockSpec((seq, LANES), lambda b, p, i: (b, p)),
            pl.BlockSpec((seq, LANES), lambda b, p, i: (b, p)),
            pl.BlockSpec((tq, FOX_HEADS), lambda b, p, i: (b * nq + i, 0)),
            pl.BlockSpec((FOX_HEADS, seq), lambda b, p, i: (0, b)),
            _const_spec((1, LANES)),
            _const_spec((1, LANES)),
        ],
        out_specs=[
            pl.BlockSpec((tq, LANES), lambda b, p, i: (b * nq + i, p)),
            pl.BlockSpec((seq, LANES), lambda b, p, i: (b, p)),
        ],
        out_shape=[jax.ShapeDtypeStruct((n, FOX_WIDTH), F32), jax.ShapeDtypeStruct((n, FOX_WIDTH), F32)],
        scratch_shapes=[pltpu.VMEM((seq, LANES), BF16), pltpu.VMEM((seq, LANES), BF16)],
        compiler_params=_params("parallel", "parallel", "arbitrary"),
        name="fox_prompt_attention",
    )(q, k, v, dcum, dcum_t, gq2, gk2)


FOX_PAGE_GROUP = 4


def _fox_sample_kernel(pt_ref, q_ref, k_ref, v_ref, f_ref, gq_ref, gk_ref, *rest, group, n_groups, t_new):
    kp, vp, fp = rest[:group], rest[group:2 * group], rest[2 * group:3 * group]
    o_ref, kn_ref = rest[3 * group:3 * group + 2]
    q_sc, kx_sc, vx_sc, cq_sc, ck_sc, s_sc, m_sc, l_sc, acc_sc, carry_sc = rest[3 * group + 2:]
    step = pl.program_id(1)
    hd, rows_h = FOX_HEAD_DIM, SUBLANES
    page = kp[0].shape[1]

    @pl.when(step == 0)
    def _():
        q, k, v, f = q_ref[0], k_ref[0], v_ref[0], f_ref[0]
        c_rows = [f[0:1, :]]
        for t in range(1, t_new):
            c_rows.append(c_rows[-1] + f[t:t + 1, :])
        c = jnp.concatenate(c_rows, axis=0)
        pad = jnp.zeros((rows_h - t_new, hd), F32)
        for h in range(FOX_HEADS):
            cs = slice(h * hd, (h + 1) * hd)
            rs = slice(h * rows_h, (h + 1) * rows_h)
            qh, kh = q[:, cs], k[:, cs]
            qn = qh * lax.rsqrt(jnp.mean(qh * qh, axis=-1, keepdims=True) + NORM_EPS) * gq_ref[...] * FOX_SCALE
            kn = kh * lax.rsqrt(jnp.mean(kh * kh, axis=-1, keepdims=True) + NORM_EPS) * gk_ref[...]
            kn_ref[0, :, cs] = kn
            q_sc[rs, :] = jnp.concatenate([qn, pad], axis=0)
            cq_sc[rs, :] = jnp.broadcast_to(
                jnp.concatenate([c[:, h:h + 1], jnp.zeros((rows_h - t_new, 1), F32)], axis=0), (rows_h, LANES))
            for j in range(t_new):
                kx_sc[j, rs, :] = jnp.broadcast_to(kn[j:j + 1, :], (rows_h, hd))
                vx_sc[j, rs, :] = jnp.broadcast_to(v[j:j + 1, cs], (rows_h, hd))
                ck_sc[j, rs, :] = jnp.broadcast_to(c[j:j + 1, h:h + 1], (rows_h, LANES))
        m_sc[...] = jnp.full(m_sc.shape, -jnp.inf, F32)
        l_sc[...] = jnp.zeros(l_sc.shape, F32)
        acc_sc[...] = jnp.zeros(acc_sc.shape, F32)
        carry_sc[...] = jnp.zeros(carry_sc.shape, F32)

    r = lax.broadcasted_iota(jnp.int32, (page, page), 0)
    c_ = lax.broadcasted_iota(jnp.int32, (page, page), 1)
    later = jnp.where(r > c_, 1.0, 0.0).astype(BF16)
    carry = carry_sc[...]
    for g in reversed(range(group)):
        ft = fp[g][0]
        suffix = _dot01_r(ft, later) + carry
        carry = carry + jnp.sum(ft, axis=-1, keepdims=True)
        for h in range(FOX_HEADS):
            rs = slice(h * rows_h, (h + 1) * rows_h)
            kh = kp[g][0, :, h, :].astype(BF16)
            s = _dot_nt(q_sc[rs, :].astype(BF16), kh)
            s_sc[rs, g * page:(g + 1) * page] = s + suffix[h:h + 1, :] + cq_sc[rs, 0:1]
    carry_sc[...] = carry

    s_all = s_sc[...]
    m_prev = m_sc[...]
    m_new = jnp.maximum(m_prev, jnp.max(s_all, axis=-1, keepdims=True))
    alpha = jnp.exp(m_prev - m_new)
    p_all = jnp.exp(s_all - m_new)
    l_sc[...] = alpha * l_sc[...] + jnp.sum(p_all, axis=-1, keepdims=True)
    m_sc[...] = m_new
    s_sc[...] = p_all
    for h in range(FOX_HEADS):
        rs = slice(h * rows_h, (h + 1) * rows_h)
        o_h = jnp.zeros((rows_h, hd), F32)
        for g in range(group):
            vh = vp[g][0, :, h, :].astype(BF16)
            o_h = o_h + _dot(s_sc[rs, g * page:(g + 1) * page].astype(BF16), vh)
        acc_sc[rs, :] = alpha[rs, :] * acc_sc[rs, :] + o_h

    @pl.when(step == n_groups - 1)
    def _():
        t_row = lax.broadcasted_iota(jnp.int32, (FOX_HEADS * rows_h, 1), 0) % rows_h
        q_all = q_sc[...]
        cq = cq_sc[:, 0:1]
        s_new = []
        for j in range(t_new):
            s_j = jnp.sum(q_all * kx_sc[j], axis=-1, keepdims=True) + cq - ck_sc[j, :, 0:1]
            s_new.append(jnp.where(t_row >= j, s_j, -jnp.inf))
        m_prev = m_sc[...]
        m_new = m_prev
        for s_j in s_new:
            m_new = jnp.maximum(m_new, s_j)
        alpha = jnp.exp(m_prev - m_new)
        l = alpha * l_sc[...]
        acc = alpha * acc_sc[...]
        for j, s_j in enumerate(s_new):
            p_j = jnp.exp(s_j - m_new)
            l = l + p_j
            acc = acc + p_j * vx_sc[j]
        o = acc / l
        for h in range(FOX_HEADS):
            o_ref[0, :, h * hd:(h + 1) * hd] = o[h * rows_h:h * rows_h + t_new, :]


def _fox_sample_attention(q, k, v, f, cache_k, cache_v, cache_ft, page_table, gq, gk):
    batch, t_new, width = q.shape
    n_pages = page_table.shape[1]
    page = cache_k.shape[1]
    group = FOX_PAGE_GROUP
    n_groups = n_pages // group
    rows = FOX_HEADS * SUBLANES

    def page_spec(shape, g):
        def index(b, s, pt):
            return (pt[b * n_pages + (n_groups - 1 - s) * group + g],) + (0,) * (len(shape) - 1)
        return pl.BlockSpec((1,) + shape[1:], index)

    tok_spec = lambda last: pl.BlockSpec((1, t_new, last), lambda b, s, pt: (b, 0, 0))
    in_specs = [tok_spec(width), tok_spec(width), tok_spec(width), tok_spec(FOX_HEADS),
                pl.BlockSpec((1, FOX_HEAD_DIM), lambda b, s, pt: (0, 0)),
                pl.BlockSpec((1, FOX_HEAD_DIM), lambda b, s, pt: (0, 0))]
    in_specs += [page_spec(cache_k.shape, g) for g in range(group)]
    in_specs += [page_spec(cache_v.shape, g) for g in range(group)]
    in_specs += [page_spec(cache_ft.shape, g) for g in range(group)]
    grid_spec = pltpu.PrefetchScalarGridSpec(
        num_scalar_prefetch=1,
        grid=(batch, n_groups),
        in_specs=in_specs,
        out_specs=[tok_spec(width), tok_spec(width)],
        scratch_shapes=[
            pltpu.VMEM((rows, FOX_HEAD_DIM), F32),
            pltpu.VMEM((t_new, rows, FOX_HEAD_DIM), F32),
            pltpu.VMEM((t_new, rows, FOX_HEAD_DIM), F32),
            pltpu.VMEM((rows, LANES), F32),
            pltpu.VMEM((t_new, rows, LANES), F32),
            pltpu.VMEM((rows, group * page), F32),
            pltpu.VMEM((rows, 1), F32),
            pltpu.VMEM((rows, 1), F32),
            pltpu.VMEM((rows, FOX_HEAD_DIM), F32),
            pltpu.VMEM((FOX_HEADS, 1), F32),
        ],
    )
    return pl.pallas_call(
        functools.partial(_fox_sample_kernel, group=group, n_groups=n_groups, t_new=t_new),
        grid_spec=grid_spec,
        out_shape=[jax.ShapeDtypeStruct(q.shape, F32), jax.ShapeDtypeStruct(q.shape, F32)],
        compiler_params=_params("parallel", "arbitrary"),
        name="fox_sample_attention",
    )(page_table.reshape(-1), q, k, v, f, gq.reshape(1, -1), gk.reshape(1, -1),
      *([cache_k] * group), *([cache_v] * group), *([cache_ft] * group))


def _fox_prompt_kernel(q_ref, k_ref, v_ref, d_ref, dt_ref, gq_ref, gk_ref, o_ref, knt_ref, vt_ref, kt_sc, v_sc, *, tq):
    pair = pl.program_id(1)
    qi = pl.program_id(2)
    first_head = lax.broadcasted_iota(jnp.int32, (1, LANES), 1) < FOX_HEAD_DIM
    n_blk = kt_sc.shape[0]

    @pl.when(qi == 0)
    def _():
        knt = _pair_head_norm(k_ref[...], gk_ref[...], first_head).T
        knt_ref[0] = knt.reshape(2, FOX_HEAD_DIM, knt.shape[1])
        for j in range(n_blk):
            kt_sc[j] = knt[:, j * tq:(j + 1) * tq].astype(BF16)
        v = v_ref[...]
        v_sc[...] = v.astype(BF16)
        vt = v.T
        vt_ref[0] = vt.reshape(2, FOX_HEAD_DIM, vt.shape[1])

    qn = _pair_head_norm(q_ref[...], gq_ref[...], first_head) * FOX_SCALE
    q_heads = (jnp.where(first_head, qn, 0.0).astype(BF16), jnp.where(first_head, 0.0, qn).astype(BF16))
    d_blk = d_ref[...]
    head_lane = lax.broadcasted_iota(jnp.int32, (1, FOX_HEADS), 1)
    d_q = [jnp.sum(jnp.where(head_lane == 2 * pair + e, d_blk, 0.0), axis=-1, keepdims=True) for e in (0, 1)]
    row = lax.broadcasted_iota(jnp.int32, (tq, tq), 0)
    col = lax.broadcasted_iota(jnp.int32, (tq, tq), 1)

    def kv_block(j, carry, diagonal):
        start = pl.multiple_of(j * tq, tq)
        kb = kt_sc[j]
        vb = v_sc[pl.ds(start, tq), :]
        new = []
        for e in (0, 1):
            m, l, acc = carry[e]
            d_k = dt_ref[pl.ds(2 * pair + e, 1), pl.ds(start, tq)]
            s = _dot(q_heads[e], kb) + (d_q[e] - d_k)
            if diagonal:
                s = jnp.where(col <= row, s, -jnp.inf)
            m_new = jnp.maximum(m, jnp.max(s, axis=-1, keepdims=True))
            alpha = jnp.exp(m - m_new)
            p = jnp.exp(s - m_new)
            l = alpha * l + jnp.sum(p, axis=-1, keepdims=True)
            acc = alpha * acc + _dot(p.astype(BF16), vb)
            new.append((m_new, l, acc))
        return tuple(new)

    init = tuple((jnp.full((tq, 1), -jnp.inf, F32), jnp.zeros((tq, 1), F32), jnp.zeros((tq, LANES), F32))
                 for _ in (0, 1))
    carry = lax.fori_loop(0, qi, lambda j, c: kv_block(j, c, False), init)
    (_, l0, acc0), (_, l1, acc1) = kv_block(qi, carry, True)
    o_ref[...] = jnp.where(first_head, acc0 / l0, acc1 / l1)


def _fox_prompt_attention(q, k, v, dcum, dcum_t, gq, gk, batch, seq):
    tq = FOX_Q_BLOCK
    nq = seq // tq
    n = batch * seq
    pairs = FOX_WIDTH // LANES
    gq2 = jnp.tile(gq, 2).reshape(1, LANES)
    gk2 = jnp.tile(gk, 2).reshape(1, LANES)
    t_shape = jax.ShapeDtypeStruct((batch, FOX_HEADS, FOX_HEAD_DIM, seq), F32)
    t_spec = pl.BlockSpec((1, 2, FOX_HEAD_DIM, seq), lambda b, p, i: (b, p, 0, 0))
    return pl.pallas_call(
        functools.partial(_fox_prompt_kernel, tq=tq),
        grid=(batch, pairs, nq),
        in_specs=[
            pl.BlockSpec((tq, LANES), lambda b, p, i: (b * nq + i, p)),
            pl.BlockSpec((seq, LANES), lambda b, p, i: (b, p)),
            pl.BlockSpec((seq, LANES), lambda b, p, i: (b, p)),
            pl.BlockSpec((tq, FOX_HEADS), lambda b, p, i: (b * nq + i, 0)),
            pl.BlockSpec((FOX_HEADS, seq), lambda b, p, i: (0, b)),
            _const_spec((1, LANES)),
            _const_spec((1, LANES)),
        ],
        out_specs=[pl.BlockSpec((tq, LANES), lambda b, p, i: (b * nq + i, p)), t_spec, t_spec],
        out_shape=[jax.ShapeDtypeStruct((n, FOX_WIDTH), F32), t_shape, t_shape],
        scratch_shapes=[pltpu.VMEM((nq, LANES, tq), BF16), pltpu.VMEM((seq, LANES), BF16)],
        compiler_params=_params("parallel", "parallel", "arbitrary"),
        name="fox_prompt_attention",
    )(q, k, v, dcum, dcum_t, gq2, gk2)


FOX_PAGE_GROUP = 8


def _fox_sample_kernel(pt_ref, q_ref, k_ref, v_ref, f_ref, gq_ref, gk_ref, *rest, group, n_groups, t_new):
    kp, vp, fp = rest[:group], rest[group:2 * group], rest[2 * group:3 * group]
    o_ref, kn_ref = rest[3 * group:3 * group + 2]
    q_sc, kn_sc, vx_sc, cq_sc, ck_sc, s_sc, m_sc, l_sc, acc_sc, carry_sc = rest[3 * group + 2:]
    step = pl.program_id(1)
    hd, rows_h = FOX_HEAD_DIM, SUBLANES
    rows, width = q_sc.shape
    page = kp[0].shape[3]
    row_head = lax.broadcasted_iota(jnp.int32, (rows, 1), 0) // rows_h

    @pl.when(step == 0)
    def _():
        own_cols = row_head == lax.broadcasted_iota(jnp.int32, (1, width), 1) // hd
        own_head = row_head == lax.broadcasted_iota(jnp.int32, (1, FOX_HEADS), 1)
        pad_rows = lambda x: jnp.concatenate([x, jnp.zeros((rows_h - t_new, x.shape[1]), F32)], axis=0)
        per_head = lambda x8: jnp.broadcast_to(x8[None], (FOX_HEADS,) + x8.shape).reshape(rows, x8.shape[1])

        def head_norm(x, g):
            xe = jnp.where(own_cols, per_head(pad_rows(x)), 0.0)
            ss = jnp.sum(xe * xe, axis=-1, keepdims=True)
            return xe * lax.rsqrt(ss / hd + NORM_EPS) * g

        q_sc[...] = head_norm(q_ref[0], gq_ref[...]) * FOX_SCALE
        kn = jnp.sum(head_norm(k_ref[0], gk_ref[...]).reshape(FOX_HEADS, rows_h, width), axis=0)
        kn_sc[...] = kn
        kn_ref[0] = kn[:t_new, :]
        f = f_ref[0]
        c_rows = [f[0:1, :]]
        for t in range(1, t_new):
            c_rows.append(c_rows[-1] + f[t:t + 1, :])
        c = jnp.concatenate(c_rows, axis=0)
        cq_sc[...] = jnp.sum(jnp.where(own_head, per_head(pad_rows(c)), 0.0), axis=-1, keepdims=True)
        v = v_ref[0]
        for j in range(t_new):
            ck_sc[j] = jnp.sum(jnp.where(own_head, c[j:j + 1, :], 0.0), axis=-1, keepdims=True)
            for h in range(FOX_HEADS):
                vx_sc[j, h * rows_h:(h + 1) * rows_h, :] = jnp.broadcast_to(v[j:j + 1, h * hd:(h + 1) * hd], (rows_h, hd))
        m_sc[...] = jnp.full(m_sc.shape, -jnp.inf, F32)
        l_sc[...] = jnp.zeros(l_sc.shape, F32)
        acc_sc[...] = jnp.zeros(acc_sc.shape, F32)
        carry_sc[...] = jnp.zeros(carry_sc.shape, F32)

    r = lax.broadcasted_iota(jnp.int32, (page, page), 0)
    c_ = lax.broadcasted_iota(jnp.int32, (page, page), 1)
    later = jnp.where(r > c_, 1.0, 0.0).astype(BF16)
    qb = q_sc[...].astype(BF16)
    cq = cq_sc[...]
    carry = carry_sc[...]
    for g in reversed(range(group)):
        ft = fp[g][0]
        suffix = _dot01_r(ft, later) + carry
        carry = carry + jnp.sum(ft, axis=-1, keepdims=True)
        bias = jnp.broadcast_to(suffix[:, None, :], (FOX_HEADS, rows_h, page)).reshape(rows, page)
        kb = kp[g][0].reshape(width, page).astype(BF16)
        s_sc[:, g * page:(g + 1) * page] = _dot(qb, kb) + bias + cq
    carry_sc[...] = carry

    s_all = s_sc[...]
    m_prev = m_sc[...]
    m_new = jnp.maximum(m_prev, jnp.max(s_all, axis=-1, keepdims=True))
    alpha = jnp.exp(m_prev - m_new)
    p_all = jnp.exp(s_all - m_new)
    l_sc[...] = alpha * l_sc[...] + jnp.sum(p_all, axis=-1, keepdims=True)
    m_sc[...] = m_new
    s_sc[...] = p_all
    for h in range(FOX_HEADS):
        rs = slice(h * rows_h, (h + 1) * rows_h)
        o_h = jnp.zeros((rows_h, hd), F32)
        for g in range(group):
            p_hg = s_sc[rs, g * page:(g + 1) * page].astype(BF16)
            o_h = o_h + _dot_nt(p_hg, vp[g][0, h].astype(BF16))
        acc_sc[rs, :] = alpha[rs, :] * acc_sc[rs, :] + o_h

    @pl.when(step == n_groups - 1)
    def _():
        t_row = lax.broadcasted_iota(jnp.int32, (rows, 1), 0) % rows_h
        q_all = q_sc[...]
        s_new = []
        for j in range(t_new):
            s_j = jnp.sum(q_all * kn_sc[j:j + 1, :], axis=-1, keepdims=True) + cq - ck_sc[j]
            s_new.append(jnp.where(t_row >= j, s_j, -jnp.inf))
        m_prev = m_sc[...]
        m_new = m_prev
        for s_j in s_new:
            m_new = jnp.maximum(m_new, s_j)
        alpha = jnp.exp(m_prev - m_new)
        l = alpha * l_sc[...]
        acc = alpha * acc_sc[...]
        for j, s_j in enumerate(s_new):
            p_j = jnp.exp(s_j - m_new)
            l = l + p_j
            acc = acc + p_j * vx_sc[j]
        o = acc / l
        for h in range(FOX_HEADS):
            o_ref[0, :, h * hd:(h + 1) * hd] = o[h * rows_h:h * rows_h + t_new, :]


def _fox_sample_attention(q, k, v, f, cache_kt, cache_vt, cache_ft, page_table, gq, gk):
    batch, t_new, width = q.shape
    n_pages = page_table.shape[1]
    page = cache_kt.shape[3]
    group = FOX_PAGE_GROUP
    n_groups = n_pages // group
    rows = FOX_HEADS * SUBLANES

    def page_spec(shape, g):
        def index(b, s, pt):
            return (pt[b * n_pages + (n_groups - 1 - s) * group + g],) + (0,) * (len(shape) - 1)
        return pl.BlockSpec((1,) + shape[1:], index)

    tok_spec = lambda last: pl.BlockSpec((1, t_new, last), lambda b, s, pt: (b, 0, 0))
    in_specs = [tok_spec(width), tok_spec(width), tok_spec(width), tok_spec(FOX_HEADS),
                pl.BlockSpec((1, width), lambda b, s, pt: (0, 0)),
                pl.BlockSpec((1, width), lambda b, s, pt: (0, 0))]
    in_specs += [page_spec(cache_kt.shape, g) for g in range(group)]
    in_specs += [page_spec(cache_vt.shape, g) for g in range(group)]
    in_specs += [page_spec(cache_ft.shape, g) for g in range(group)]
    grid_spec = pltpu.PrefetchScalarGridSpec(
        num_scalar_prefetch=1,
        grid=(batch, n_groups),
        in_specs=in_specs,
        out_specs=[tok_spec(width), tok_spec(width)],
        scratch_shapes=[
            pltpu.VMEM((rows, width), F32),
            pltpu.VMEM((SUBLANES, width), F32),
            pltpu.VMEM((t_new, rows, FOX_HEAD_DIM), F32),
            pltpu.VMEM((rows, 1), F32),
            pltpu.VMEM((t_new, rows, 1), F32),
            pltpu.VMEM((rows, group * page), F32),
            pltpu.VMEM((rows, 1), F32),
            pltpu.VMEM((rows, 1), F32),
            pltpu.VMEM((rows, FOX_HEAD_DIM), F32),
            pltpu.VMEM((FOX_HEADS, 1), F32),
        ],
    )
    heads_tiled = lambda g: jnp.tile(g, FOX_HEADS).reshape(1, width)
    return pl.pallas_call(
        functools.partial(_fox_sample_kernel, group=group, n_groups=n_groups, t_new=t_new),
        grid_spec=grid_spec,
        out_shape=[jax.ShapeDtypeStruct(q.shape, F32), jax.ShapeDtypeStruct(q.shape, F32)],
        compiler_params=_params("parallel", "arbitrary"),
        name="fox_sample_attention",
    )(page_table.reshape(-1), q, k, v, f, heads_tiled(gq), heads_tiled(gk),
      *([cache_kt] * group), *([cache_vt] * group), *([cache_ft] * group))


def _fox_out_body(rows, consts, outs, is_sample, tile_index):
    (x_ref, o_ref, gate_ref), (w_ref,), (y_ref,) = rows, consts, outs
    gated = (o_ref[...] * jax.nn.sigmoid(gate_ref[...])).astype(BF16)
    y_ref[...] = x_ref[...] + _dot(gated, w_ref[...])


def _fox_layer(x, g, w_in, b_f, gq, gk, w_out, cache_k, cache_v, cache_f, page_table, dims):
    batch, seq, dec_batch, dec_seq = dims
    q, k, v, gate, f, ft = _fox_project(x, g, w_in, b_f)
    dcum, dcum_t = _fox_cumsum(f[0], ft[0], batch, seq)
    o_p, knt_p, vt_p = _fox_prompt_attention(q[0], k[0], v[0], dcum, dcum_t, gq, gk, batch, seq)
    per_seq = lambda a: a.reshape(dec_batch, dec_seq, a.shape[-1])
    o_s, kn_s = _fox_sample_attention(per_seq(q[1]), per_seq(k[1]), per_seq(v[1]), per_seq(f[1]),
                                      jnp.transpose(cache_k, (0, 2, 3, 1)), jnp.transpose(cache_v, (0, 2, 3, 1)),
                                      jnp.swapaxes(cache_f, 1, 2), page_table, gq, gk)
    o = (o_p, o_s.reshape(-1, FOX_WIDTH))
    (y,) = _row_call(_fox_out_body, [x, o, gate], [w_out.astype(BF16)], [(w_out.shape[1], F32, False)], "fox_out")
    heads = lambda a, b, t: a.reshape(b, t, FOX_HEADS, FOX_HEAD_DIM)
    seq_first = lambda a: jnp.transpose(a, (0, 3, 1, 2))
    state = (seq_first(knt_p), seq_first(vt_p), f[0].reshape(batch, seq, FOX_HEADS),
             heads(kn_s, dec_batch, dec_seq), heads(v[1], dec_batch, dec_seq),
             f[1].reshape(dec_batch, dec_seq, FOX_HEADS))
    return y, state


RWKV_HEAD_DIM = 64
RWKV_GN_EPS = 64e-5
RWKV_GATE_LORA_PAD = 256
RWKV_CHUNK = 64
RWKV_CHUNKS_PER_STEP = 8


def _rwkv_proj_body(rows, consts, outs, is_sample, tile_index, *, seq, dec_seq):
    x_ref, prev_ref, start_ref = rows
    (g_ref, mu_ref, w_rkv_ref, w0_ref, w1_ref, w2_ref, a0_ref, a1_ref, a2_ref, g1_ref, g2_ref) = consts
    r_ref, k_ref, v_ref, ld_ref, a_ref, gate_ref, xn_ref, last_ref = outs
    tile = x_ref.shape[0]
    xn = _rmsnorm(x_ref[...], g_ref[...])
    rolled = pltpu.roll(xn, 1, 0)
    row = lax.broadcasted_iota(jnp.int32, (tile, 1), 0)
    if is_sample:
        xn_ref[...] = xn
        x_prev = jnp.where(row % dec_seq == 0, start_ref[...], rolled)
    else:
        before = _rmsnorm(prev_ref[SUBLANES - 1:SUBLANES, :], g_ref[...])
        before = jnp.where((tile_index * tile) % seq == 0, 0.0, before)
        x_prev = jnp.where(row == 0, before, rolled)

        @pl.when(((tile_index + 1) * tile) % seq == 0)
        def _():
            last_ref[pl.ds((tile_index * tile) // seq, 1), :] = xn[tile - 1:tile, :]

    xx = x_prev - xn
    mix = lambda i: (xn + xx * mu_ref[i:i + 1, :]).astype(BF16)
    r_ref[...] = _dot(mix(0), w_rkv_ref[0])
    k_ref[...] = _dot(mix(2), w_rkv_ref[1])
    v_ref[...] = _dot(mix(3), w_rkv_ref[2])
    wl = w0_ref[...] + _dot(jnp.tanh(_dot(mix(1), w1_ref[...])).astype(BF16), w2_ref[...])
    softplus_neg = jnp.maximum(-wl, 0.0) + jnp.log1p(jnp.exp(-jnp.abs(wl)))
    ld_ref[...] = -jnp.exp(-softplus_neg - 0.5)
    a_ref[...] = jax.nn.sigmoid(a0_ref[...] + _dot(_dot(mix(4), a1_ref[...]).astype(BF16), a2_ref[...]))
    gate_ref[...] = _dot(jax.nn.sigmoid(_dot(mix(5), g1_ref[...])).astype(BF16), g2_ref[...])


def _dot3(a, b):
    a_hi = a.astype(BF16)
    a_lo = (a - a_hi.astype(F32)).astype(BF16)
    b_hi = b.astype(BF16)
    b_lo = (b - b_hi.astype(F32)).astype(BF16)
    return _dot(a_hi, b_hi) + _dot(a_hi, b_lo) + _dot(a_lo, b_hi)


def _pair_sums(x, first_head):
    s0 = jnp.sum(jnp.where(first_head, x, 0.0), axis=-1, keepdims=True)
    s1 = jnp.sum(jnp.where(first_head, 0.0, x), axis=-1, keepdims=True)
    return jnp.where(first_head, s0, s1)


def _rwkv_chunk_terms(chunks, kk_w, ka_w):
    length = chunks[0][0].shape[0]
    first_head = lax.broadcasted_iota(jnp.int32, (1, LANES), 1) < RWKV_HEAD_DIM
    head_masks = (first_head, jnp.logical_not(first_head))
    ti = lax.broadcasted_iota(jnp.int32, (length, length), 0)
    si = lax.broadcasted_iota(jnp.int32, (length, length), 1)
    strict, incl = si < ti, si <= ti
    lower01 = jnp.where(incl, 1.0, 0.0).astype(BF16)
    levels = max(1, math.ceil(math.log2(length)))
    insts = [(c, e) for c in range(len(chunks)) for e in (0, 1)]

    cl = [_dot01(lower01, ld) for (_, _, _, ld, _) in chunks]
    pre = []
    for (r, k, v, ld, a), cl_c in zip(chunks, cl):
        kk = k * kk_w
        kkn = kk / jnp.maximum(jnp.sqrt(_pair_sums(kk * kk, first_head)), 1e-12)
        k2 = k * (1.0 + (a - 1.0) * ka_w)
        aa, bb = -kkn, kkn * a
        cl_last = cl_c[length - 1:length, :]
        p_out = jnp.exp(-cl_c)
        p_end = jnp.exp(cl_last - cl_c)
        pre.append(dict(
            k2=k2, vb=v.astype(BF16), r_t=r * jnp.exp(cl_c), a_t=aa * jnp.exp(cl_c - ld),
            rhs=jnp.concatenate([bb * p_out, k2 * p_out], axis=0).astype(BF16),
            b_end=(bb * p_end).astype(BF16), k_end=(k2 * p_end).astype(BF16), decay_end=jnp.exp(cl_last)))
    a_e = [jnp.where(head_masks[e], pre[c]["a_t"], 0.0) for c, e in insts]
    r_e = [jnp.where(head_masks[e], pre[c]["r_t"], 0.0) for c, e in insts]
    mm = [_dot_nt(jnp.concatenate([a_e[i], r_e[i]], axis=0).astype(BF16), pre[c]["rhs"])
          for i, (c, e) in enumerate(insts)]
    m_ak = [jnp.where(strict, m[:length, length:], 0.0).astype(BF16) for m in mm]
    m_rb = [jnp.where(incl, m[length:, :length], 0.0).astype(BF16) for m in mm]
    m_rk = [jnp.where(incl, m[length:, length:], 0.0).astype(BF16) for m in mm]
    power = [jnp.where(strict, m[:length, :length], 0.0) for m in mm]
    x = [jnp.concatenate([a_e[i], _dot(m_ak[i], pre[c]["vb"])], axis=1) for i, (c, e) in enumerate(insts)]
    for lvl in range(levels):
        pb = [p_i.astype(BF16) for p_i in power]
        x = [x_i + _dot(p_i, x_i.astype(BF16)) for x_i, p_i in zip(x, pb)]
        if lvl + 1 < levels:
            power = [_dot(p_i, p_i) for p_i in pb]
    w_e = [x_i[:, :LANES] for x_i in x]
    u0_e = [x_i[:, LANES:] for x_i in x]
    yr_e = [r_e[i] + _dot(m_rb[i], w_e[i].astype(BF16)) for i in range(len(insts))]
    y0_e = [_dot(m_rb[i], u0_e[i].astype(BF16)) + _dot(m_rk[i], pre[c]["vb"]) for i, (c, e) in enumerate(insts)]
    row = lax.broadcasted_iota(jnp.int32, (LANES, LANES), 0)
    col = lax.broadcasted_iota(jnp.int32, (LANES, LANES), 1)
    same_head = (row < RWKV_HEAD_DIM) == (col < RWKV_HEAD_DIM)
    out = []
    for c in range(len(chunks)):
        i0, i1 = 2 * c, 2 * c + 1
        w = (w_e[i0] + w_e[i1]).astype(BF16)
        u0 = jnp.where(first_head, u0_e[i0], u0_e[i1]).astype(BF16)
        g = jnp.where(same_head, jnp.where(row == col, pre[c]["decay_end"], 0.0) + _dot_tn(pre[c]["b_end"], w), 0.0)
        h = jnp.where(same_head, _dot_tn(pre[c]["b_end"], u0) + _dot_tn(pre[c]["k_end"], pre[c]["vb"]), 0.0)
        out.append((g, h, yr_e[i0] + yr_e[i1], jnp.where(first_head, y0_e[i0], y0_e[i1]), pre[c]["k2"]))
    return out


def _rwkv_finish(y, r, k2, v, gate, rk_w, gnw, gnb):
    first_head = lax.broadcasted_iota(jnp.int32, (1, LANES), 1) < RWKV_HEAD_DIM
    mean = _pair_sums(y, first_head) / RWKV_HEAD_DIM
    yc = y - mean
    var = _pair_sums(yc * yc, first_head) / RWKV_HEAD_DIM
    yn = yc * lax.rsqrt(var + RWKV_GN_EPS) * gnw + gnb
    bonus = _pair_sums(r * k2 * rk_w, first_head) * v
    return (yn + bonus) * gate


def _rwkv_scan_kernel(r_ref, k_ref, v_ref, ld_ref, a_ref, gate_ref, kk_ref, ka_ref, rk_ref, gnw_ref, gnb_ref,
                      z_ref, s_ref, state_sc, *, length, n_inst):
    step = pl.program_id(2)

    @pl.when(step == 0)
    def _():
        state_sc[...] = jnp.zeros(state_sc.shape, F32)

    state = state_sc[...]
    rows = [slice(i * length, (i + 1) * length) for i in range(n_inst)]
    chunks = [(r_ref[rs, :], k_ref[rs, :], v_ref[rs, :], ld_ref[rs, :], a_ref[rs, :]) for rs in rows]
    terms = _rwkv_chunk_terms(chunks, kk_ref[...], ka_ref[...])
    for rs, (r, k, v, ld, a), (g, h, yr, y0, k2) in zip(rows, chunks, terms):
        y = _dot(yr.astype(BF16), state.astype(BF16)) + y0
        state = _dot3(g, state) + h
        z_ref[rs, :] = _rwkv_finish(y, r, k2, v, gate_ref[rs, :], rk_ref[...], gnw_ref[...], gnb_ref[...])
    state_sc[...] = state

    @pl.when(step == pl.num_programs(2) - 1)
    def _():
        st = state.T
        s_ref[0, 0] = st[:RWKV_HEAD_DIM, :RWKV_HEAD_DIM]
        s_ref[0, 1] = st[RWKV_HEAD_DIM:, RWKV_HEAD_DIM:]


def _rwkv_scan(r, k, v, ld, a, gate, params, n_seq, seq):
    d = r.shape[1]
    pairs = d // LANES
    heads = d // RWKV_HEAD_DIM
    rows = RWKV_CHUNKS_PER_STEP * RWKV_CHUNK
    steps = seq // rows
    row_spec = pl.BlockSpec((rows, LANES), lambda b, p, s: (b * steps + s, p))
    par_spec = pl.BlockSpec((1, LANES), lambda b, p, s: (0, p))
    st_spec = pl.BlockSpec((1, 2, RWKV_HEAD_DIM, RWKV_HEAD_DIM), lambda b, p, s: (b, p, 0, 0))
    return pl.pallas_call(
        functools.partial(_rwkv_scan_kernel, length=RWKV_CHUNK, n_inst=RWKV_CHUNKS_PER_STEP),
        grid=(n_seq, pairs, steps),
        in_specs=[row_spec] * 6 + [par_spec] * 5,
        out_specs=[row_spec, st_spec],
        out_shape=[jax.ShapeDtypeStruct(r.shape, F32),
                   jax.ShapeDtypeStruct((n_seq, heads, RWKV_HEAD_DIM, RWKV_HEAD_DIM), F32)],
        scratch_shapes=[pltpu.VMEM((LANES, LANES), F32)],
        compiler_params=_params("parallel", "parallel", "arbitrary"),
        name="rwkv_scan_prompt",
    )(r, k, v, ld, a, gate, *params)


def _rwkv_sample_kernel(r_ref, k_ref, v_ref, ld_ref, a_ref, gate_ref, kk_ref, ka_ref, rk_ref, gnw_ref, gnb_ref,
                        s0_ref, z_ref, s_ref, y_sc, *, steps):
    n = RWKV_HEAD_DIM
    s_ref[...] = s0_ref[...]
    kk_w, ka_w, rk_w, gnw, gnb = kk_ref[0], ka_ref[0], rk_ref[0], gnw_ref[0], gnb_ref[0]
    for t in range(steps):
        r, k, v, a = r_ref[t, 0], k_ref[t, 0], v_ref[t, 0], a_ref[t, 0]
        decay = jnp.exp(ld_ref[t, 0])
        kk = k * kk_w
        kkn = kk / jnp.maximum(jnp.sqrt(jnp.sum(kk * kk, axis=0, keepdims=True)), 1e-12)
        k2 = k * (1.0 + (a - 1.0) * ka_w)
        aa, bb = -kkn, kkn * a

        def value_row(i, carry):
            s_i = s_ref[0, i]
            sa = jnp.sum(s_i * aa, axis=0, keepdims=True)
            v_i = v_ref[t, 0, pl.ds(i, 1), :]
            s_i = s_i * decay + sa * bb + v_i * k2
            s_ref[0, i] = s_i
            y_sc[pl.ds(i, 1), :] = jnp.sum(s_i * r, axis=0, keepdims=True)
            return carry

        lax.fori_loop(0, n, value_row, 0, unroll=4)
        y = y_sc[...]
        mean = jnp.mean(y, axis=0, keepdims=True)
        yc = y - mean
        var = jnp.mean(yc * yc, axis=0, keepdims=True)
        yn = yc * lax.rsqrt(var + RWKV_GN_EPS) * gnw + gnb
        bonus = jnp.sum(r * k2 * rk_w, axis=0, keepdims=True) * v
        z_ref[t, 0] = (yn + bonus) * gate_ref[t, 0]


def _rwkv_sample(r, k, v, ld, a, gate, s0, params, dec_batch, dec_seq):
    d = r.shape[1]
    n = RWKV_HEAD_DIM
    heads = d // n
    lanes_last = lambda t: jnp.transpose(t.reshape(dec_batch, dec_seq, heads, n), (1, 2, 3, 0))
    tok_spec = pl.BlockSpec((dec_seq, 1, n, dec_batch), lambda h: (0, h, 0, 0))
    par_spec = pl.BlockSpec((1, n, 1), lambda h: (h, 0, 0))
    st_spec = pl.BlockSpec((1, n, n, dec_batch), lambda h: (h, 0, 0, 0))
    z, s = pl.pallas_call(
        functools.partial(_rwkv_sample_kernel, steps=dec_seq),
        grid=(heads,),
        in_specs=[tok_spec] * 6 + [par_spec] * 5 + [st_spec],
        out_specs=[tok_spec, st_spec],
        out_shape=[jax.ShapeDtypeStruct((dec_seq, heads, n, dec_batch), F32),
                   jax.ShapeDtypeStruct((heads, n, n, dec_batch), F32)],
        scratch_shapes=[pltpu.VMEM((n, dec_batch), F32)],
        compiler_params=_params("parallel"),
        name="rwkv_scan_sample",
    )(*[lanes_last(t) for t in (r, k, v, ld, a, gate)], *[p.reshape(heads, n, 1) for p in params],
      jnp.transpose(s0, (1, 2, 3, 0)))
    return jnp.transpose(z, (3, 0, 1, 2)).reshape(dec_batch * dec_seq, d), jnp.transpose(s, (3, 0, 1, 2))


def _rwkv_out_body(rows, consts, outs, is_sample, tile_index):
    (x_ref, z_ref), (w_ref,), (y_ref,) = rows, consts, outs
    y_ref[...] = x_ref[...] + _dot(z_ref[...].astype(BF16), w_ref[...])


def _rwkv_layer(x, g, shift_state, wkv_state, mu, w0, w1, w2, a0, a1, a2, g1, g2, kk_w, ka_w, rk_w, w_rkv, w_o,
                gn_w, gn_b, dims):
    batch, seq, dec_batch, dec_seq = dims
    d = mu.shape[1]
    heads = d // RWKV_HEAD_DIM
    row = lambda t: t.reshape(1, d)
    starts = jnp.concatenate([shift_state[:, None, :], jnp.zeros((dec_batch, dec_seq - 1, d), F32)], axis=1)
    rank = g1.shape[1]
    g1p = jnp.pad(g1, ((0, 0), (0, RWKV_GATE_LORA_PAD - rank))).astype(BF16)
    g2p = jnp.pad(g2, ((0, RWKV_GATE_LORA_PAD - rank), (0, 0))).astype(BF16)
    consts = [row(g), mu, w_rkv.astype(BF16), row(w0), w1.astype(BF16), w2.astype(BF16), row(a0), a1.astype(BF16),
              a2.astype(BF16), g1p, g2p]
    outs = [(d, F32, False)] * 6 + [(d, F32, "sample"), ((batch, d), F32, "whole")]
    r, k, v, ld, a, gate, xn, last_p = _row_call(
        functools.partial(_rwkv_proj_body, seq=seq, dec_seq=dec_seq),
        [x, (x[0], None, "prev"), (None, starts.reshape(-1, d))], consts, outs, "rwkv_proj", tile_div=2)
    params = [row(t) for t in (kk_w, ka_w, rk_w.reshape(-1), gn_w, gn_b)]
    z_p, s_p = _rwkv_scan(r[0], k[0], v[0], ld[0], a[0], gate[0], params, batch, seq)
    z_s, s_s = _rwkv_sample(r[1], k[1], v[1], ld[1], a[1], gate[1], wkv_state, params, dec_batch, dec_seq)
    (y,) = _row_call(_rwkv_out_body, [x, (z_p, z_s)], [w_o.astype(BF16)], [(d, F32, False)], "rwkv_out")
    shift_s = xn[1].reshape(dec_batch, dec_seq, d)[:, dec_seq - 1]
    return y, (last_p, s_p, shift_s, s_s)


MLSTM_HEADS = 4
MLSTM_QK_DIM = 128
MLSTM_V_DIM = 256
MLSTM_CHUNK = 64
MLSTM_GATE_CAP = 15.0
MLSTM_CHUNKS_PER_STEP = 8
MLSTM_SEQS_PER_STEP = 16
MLSTM_NO_INPUT = -1e30


def _mlstm_proj_body(rows, consts, outs, is_sample, tile_index):
    (x_ref,), (g_ref, w_ref, wg_ref, bg_ref) = rows, consts
    q_ref, k_ref, v_ref, og_ref, ig_ref, igt_ref = outs
    hk = MLSTM_HEADS * MLSTM_QK_DIM
    hv = MLSTM_HEADS * MLSTM_V_DIM
    xn = _rmsnorm(x_ref[...], g_ref[...]).astype(BF16)
    q_ref[...] = _dot(xn, w_ref[:, :hk]) * MLSTM_QK_DIM ** -0.5
    k_ref[...] = _dot(xn, w_ref[:, hk:2 * hk])
    v_ref[...] = _dot(xn, w_ref[:, 2 * hk:2 * hk + hv])
    og_ref[...] = _dot(xn, w_ref[:, 2 * hk + hv:])
    gates = MLSTM_GATE_CAP * jnp.tanh((_dot(xn, wg_ref[...]) + bg_ref[...]) / MLSTM_GATE_CAP)
    is_forget = lax.broadcasted_iota(jnp.int32, (1, LANES), 1) >= MLSTM_HEADS
    ig = jnp.where(is_forget, _log_sigmoid(gates), gates)
    ig_ref[...] = ig[:, :2 * MLSTM_HEADS]
    igt_ref[...] = ig.T[:2 * MLSTM_HEADS, :]


def _mlstm_chunks(chunks, states, head, gn, chained):
    length = chunks[0][0].shape[0]
    nh = MLSTM_HEADS
    count = len(chunks)
    ti = lax.broadcasted_iota(jnp.int32, (length, length), 0)
    si = lax.broadcasted_iota(jnp.int32, (length, length), 1)
    causal = si <= ti
    lower01 = jnp.where(causal, 1.0, 0.0).astype(BF16)
    upper01 = jnp.where(ti <= si, 1.0, 0.0).astype(BF16)
    lane_g = lax.broadcasted_iota(jnp.int32, (1, 2 * nh), 1)
    sub_g = lax.broadcasted_iota(jnp.int32, (2 * nh, 1), 0)
    col = lambda x, j: jnp.sum(jnp.where(lane_g == j, x, 0.0), axis=-1, keepdims=True)
    rowv = lambda x, j: jnp.sum(jnp.where(sub_g == j, x, 0.0), axis=0, keepdims=True)
    last = slice(length - 1, length)

    cum_c = [_dot01(lower01, ch[4]) for ch in chunks]
    cum_r = [_dot01_r(ch[5], upper01) for ch in chunks]
    b_c = [col(x, nh + head) for x in cum_c]
    li_c = [col(ch[4], head) for ch in chunks]
    logw = [jnp.where(causal, b - rowv(xr, nh + head) + rowv(ch[5], head), -jnp.inf)
            for b, xr, ch in zip(b_c, cum_r, chunks)]
    max_logw = [jnp.max(x, axis=-1, keepdims=True) for x in logw]
    qb = [ch[0].astype(BF16) for ch in chunks]
    kb = [ch[1].astype(BF16) for ch in chunks]
    vb = [ch[2].astype(BF16) for ch in chunks]
    qk = [_dot_nt(a, b) for a, b in zip(qb, kb)]
    m_in, m_t = [], []
    for i in range(count):
        m_prev = states[i][2] if (not chained or i == 0) else m_t[i - 1][last, :]
        m_in.append(m_prev)
        m_t.append(jnp.maximum(m_prev + b_c[i], max_logw[i]))
    inter = [m + b for m, b in zip(m_in, b_c)]
    inter_scale = [jnp.exp(a - b) for a, b in zip(inter, m_t)]
    sw = [jnp.exp(lw - mt) * s for lw, mt, s in zip(logw, m_t, qk)]
    swv = [_dot(s.astype(BF16), v) for s, v in zip(sw, vb)]
    m_new = [mt[last, :] for mt in m_t]
    carry_scale = [jnp.exp(a[last, :] - b) for a, b in zip(inter, m_new)]
    w_s = [jnp.exp(b[last, :] - b + li - mn) for b, li, mn in zip(b_c, li_c, m_new)]
    kv = [_dot_tn((ch[2] * w).astype(BF16), k) for ch, w, k in zip(chunks, w_s, kb)]
    ksum = [jnp.sum(w * ch[1], axis=0, keepdims=True) for ch, w in zip(chunks, w_s)]
    c_in, n_in, after = [], [], []
    for i in range(count):
        c_prev, n_prev = (states[i][0], states[i][1]) if (not chained or i == 0) else after[i - 1][:2]
        c_in.append(c_prev)
        n_in.append(n_prev)
        after.append((carry_scale[i] * c_prev + kv[i], carry_scale[i] * n_prev + ksum[i], m_new[i]))
    outs = []
    for i, ch in enumerate(chunks):
        num = inter_scale[i] * _dot_nt(qb[i], c_in[i].astype(BF16)) + swv[i]
        den = inter_scale[i] * jnp.sum(ch[0] * n_in[i], axis=-1, keepdims=True) + jnp.sum(sw[i], axis=-1, keepdims=True)
        h = num / jnp.maximum(jnp.abs(den), jnp.exp(-m_t[i]))
        h = h * lax.rsqrt(jnp.mean(h * h, axis=-1, keepdims=True) + NORM_EPS)
        outs.append(h * gn * jax.nn.sigmoid(ch[3]))
    return outs, after


def _mlstm_scan_kernel(q_ref, k_ref, v_ref, og_ref, ig_ref, igt_ref, gn_ref, c0_ref, nm0_ref,
                       h_ref, c_ref, nm_ref, c_sc, nm_sc, *, length, n_inst, chained):
    head = pl.program_id(1)
    step = pl.program_id(2)
    rows = [slice(i * length, (i + 1) * length) for i in range(n_inst)]
    chunks = [(q_ref[rs, :], k_ref[rs, :], v_ref[rs, :], og_ref[rs, :], ig_ref[rs, :], igt_ref[:, rs]) for rs in rows]
    pack_nm = lambda n, m: jnp.concatenate([n, jnp.broadcast_to(m, n.shape)], axis=0)
    if chained:
        @pl.when(step == 0)
        def _():
            c_sc[...] = c0_ref[0, 0]
            nm_sc[...] = nm0_ref[0, 0]
        states = [(c_sc[...], nm_sc[0:1, :], nm_sc[1:2, 0:1])]
    else:
        states = [(c0_ref[i, 0], nm0_ref[i, 0, 0:1, :], nm0_ref[i, 0, 1:2, 0:1]) for i in range(n_inst)]
    outs, after = _mlstm_chunks(chunks, states, head, gn_ref[...], chained)
    for rs, h in zip(rows, outs):
        h_ref[rs, :] = h
    if chained:
        c, n, m = after[-1]
        c_sc[...] = c
        nm_sc[...] = pack_nm(n, m)

        @pl.when(step == pl.num_programs(2) - 1)
        def _():
            c_ref[0, 0] = c
            nm_ref[0, 0] = pack_nm(n, m)
    else:
        for i, (c, n, m) in enumerate(after):
            c_ref[i, 0] = c
            nm_ref[i, 0] = pack_nm(n, m)


def _mlstm_scan(q, k, v, og, ig, igt, gn, c0, nm0, n_seq, seq, length, n_inst, chained):
    dk, dv, nh = MLSTM_QK_DIM, MLSTM_V_DIM, MLSTM_HEADS
    rows = n_inst * length
    if chained:
        steps = seq // rows
        grid = (n_seq, nh, steps)
        r_idx = lambda b, h, s: b * steps + s
        sb = 1
    else:
        assert seq == length
        grid = (n_seq // n_inst, nh, 1)
        r_idx = lambda b, h, s: b
        sb = n_inst
    c_spec = pl.BlockSpec((sb, 1, dv, dk), lambda b, h, s: (b, h, 0, 0))
    nm_spec = pl.BlockSpec((sb, 1, 2, dk), lambda b, h, s: (b, h, 0, 0))
    return pl.pallas_call(
        functools.partial(_mlstm_scan_kernel, length=length, n_inst=n_inst, chained=chained),
        grid=grid,
        in_specs=[
            pl.BlockSpec((rows, dk), lambda b, h, s: (r_idx(b, h, s), h)),
            pl.BlockSpec((rows, dk), lambda b, h, s: (r_idx(b, h, s), h)),
            pl.BlockSpec((rows, dv), lambda b, h, s: (r_idx(b, h, s), h)),
            pl.BlockSpec((rows, dv), lambda b, h, s: (r_idx(b, h, s), h)),
            pl.BlockSpec((rows, 2 * nh), lambda b, h, s: (r_idx(b, h, s), 0)),
            pl.BlockSpec((2 * nh, rows), lambda b, h, s: (0, r_idx(b, h, s))),
            pl.BlockSpec((1, dv), lambda b, h, s: (0, h)),
            c_spec, nm_spec,
        ],
        out_specs=[pl.BlockSpec((rows, dv), lambda b, h, s: (r_idx(b, h, s), h)), c_spec, nm_spec],
        out_shape=[jax.ShapeDtypeStruct(v.shape, F32), jax.ShapeDtypeStruct((n_seq, nh, dv, dk), F32),
                   jax.ShapeDtypeStruct((n_seq, nh, 2, dk), F32)],
        scratch_shapes=[pltpu.VMEM((dv, dk), F32), pltpu.VMEM((2, dk), F32)],
        compiler_params=_params("parallel", "parallel", "arbitrary"),
        name="mlstm_scan_prompt" if chained else "mlstm_scan_sample",
    )(q, k, v, og, ig, igt, gn, c0, nm0)


def _mlstm_out_body(rows, consts, outs, is_sample, tile_index):
    (x_ref, h_ref), (w_ref,), (y_ref,) = rows, consts, outs
    y_ref[...] = x_ref[...] + _dot(h_ref[...].astype(BF16), w_ref[...])


def _mlstm_layer(x, g, c_state, n_state, m_state, w_in, b_if, gn, w_out, dims):
    batch, seq, dec_batch, dec_seq = dims
    dk, dv, nh = MLSTM_QK_DIM, MLSTM_V_DIM, MLSTM_HEADS
    d = w_in.shape[0]
    n_main = 2 * nh * dk + 2 * nh * dv
    w_main = w_in[:, :n_main].astype(BF16)
    w_g = jnp.pad(w_in[:, n_main:], ((0, 0), (0, LANES - 2 * nh))).astype(BF16)
    b_g = jnp.pad(b_if, (0, LANES - 2 * nh)).reshape(1, LANES)
    outs = [(nh * dk, F32, False), (nh * dk, F32, False), (nh * dv, F32, False), (nh * dv, F32, False),
            (2 * nh, F32, False), (2 * nh, F32, True)]
    q, k, v, og, ig, igt = _row_call(_mlstm_proj_body, [x], [g.reshape(1, d), w_main, w_g, b_g], outs, "mlstm_proj")
    gn2 = gn.reshape(1, nh * dv)
    zeros_c = jnp.zeros((batch, nh, dv, dk), F32)
    zeros_nm = jnp.zeros((batch, nh, 2, dk), F32)
    h_p, c_p, nm_p = _mlstm_scan(q[0], k[0], v[0], og[0], ig[0], igt[0], gn2, zeros_c, zeros_nm, batch, seq,
                                 MLSTM_CHUNK, MLSTM_CHUNKS_PER_STEP, True)
    extra = SUBLANES - dec_seq
    pad_rows = lambda t: jnp.pad(t.reshape(dec_batch, dec_seq, -1), ((0, 0), (0, extra), (0, 0))).reshape(dec_batch * SUBLANES, -1)
    ig_s = ig[1].reshape(dec_batch, dec_seq, 2 * nh)
    pad_gate = jnp.concatenate([jnp.full((dec_batch, extra, nh), MLSTM_NO_INPUT, F32), jnp.zeros((dec_batch, extra, nh), F32)], axis=-1)
    ig_s = jnp.concatenate([ig_s, pad_gate], axis=1).reshape(dec_batch * SUBLANES, 2 * nh)
    nm_s = jnp.stack([n_state, jnp.broadcast_to(m_state[..., None], n_state.shape)], axis=2)
    h_s, c_s, nm_s = _mlstm_scan(pad_rows(q[1]), pad_rows(k[1]), pad_rows(v[1]), pad_rows(og[1]), ig_s, ig_s.T, gn2,
                                 c_state, nm_s, dec_batch, SUBLANES, SUBLANES, MLSTM_SEQS_PER_STEP, False)
    h_s = h_s.reshape(dec_batch, SUBLANES, nh * dv)[:, :dec_seq].reshape(-1, nh * dv)
    (y,) = _row_call(_mlstm_out_body, [x, (h_p, h_s)], [w_out.astype(BF16)], [(d, F32, False)], "mlstm_out")
    state = (c_p, nm_p[:, :, 0], nm_p[:, :, 1, 0], c_s, nm_s[:, :, 0], nm_s[:, :, 1, 0])
    return y, state


def _final_norm_body(rows, consts, outs, is_sample, tile_index):
    outs[0][...] = _rmsnorm(rows[0][...], consts[0][...])


def _final_norm(x, g):
    d = g.shape[0]
    return _row_call(_final_norm_body, [x], [g.reshape(1, d)], [(d, F32, False)], "final_norm")[0]


N_MIXERS = 3


def kernel(x_prompt, x_sample,
           cache_k_l0, cache_v_l0, cache_f_l0,
           state_shift_l1, state_wkv_l1,
           state_c_l2, state_n_l2, state_m_l2,
           cache_k_l3, cache_v_l3, cache_f_l3,
           page_table,
           norm_g, final_g, ffn_w_in, ffn_w_out,
           fox_w_in, fox_b_f, fox_gq, fox_gk, fox_w_out,
           rw_mu, rw_w0, rw_w1, rw_w2, rw_a0, rw_a1, rw_a2, rw_g1, rw_g2,
           rw_kk, rw_ka, rw_rk, rw_w_rkv, rw_w_o, rw_gn_w, rw_gn_b,
           ml_w_in, ml_b_if, ml_gn, ml_w_out):
    batch, seq, d = x_prompt.shape
    dec_batch, dec_seq, _ = x_sample.shape
    dims = (batch, seq, dec_batch, dec_seq)
    depth = norm_g.shape[0]
    x = (x_prompt.reshape(batch * seq, d), x_sample.reshape(dec_batch * dec_seq, d))
    fox_caches = ((cache_k_l0, cache_v_l0, cache_f_l0), (cache_k_l3, cache_v_l3, cache_f_l3))
    rwkv_states = ((state_shift_l1, state_wkv_l1),)
    mlstm_states = ((state_c_l2, state_n_l2, state_m_l2),)
    w_in_bf, w_out_bf = ffn_w_in.astype(BF16), ffn_w_out.astype(BF16)
    layer_state = []
    for i in range(depth):
        kind, j = i % N_MIXERS, i // N_MIXERS
        x = _ffn(x, norm_g[i, 0], w_in_bf[i, 0], w_out_bf[i, 0])
        if kind == 0:
            ck, cv, cf = fox_caches[j]
            x, st = _fox_layer(x, norm_g[i, 1], fox_w_in[j], fox_b_f[j], fox_gq[j], fox_gk[j], fox_w_out[j],
                               ck, cv, cf, page_table, dims)
        elif kind == 1:
            sh_in, wkv_in = rwkv_states[j]
            x, st = _rwkv_layer(x, norm_g[i, 1], sh_in, wkv_in, rw_mu[j], rw_w0[j], rw_w1[j], rw_w2[j], rw_a0[j],
                                rw_a1[j], rw_a2[j], rw_g1[j], rw_g2[j], rw_kk[j], rw_ka[j], rw_rk[j], rw_w_rkv[j],
                                rw_w_o[j], rw_gn_w[j], rw_gn_b[j], dims)
        else:
            c_in, n_in, m_in = mlstm_states[j]
            x, st = _mlstm_layer(x, norm_g[i, 1], c_in, n_in, m_in, ml_w_in[j], ml_b_if[j], ml_gn[j], ml_w_out[j], dims)
        layer_state.append(st)
        x = _ffn(x, norm_g[i, 2], w_in_bf[i, 1], w_out_bf[i, 1])
    y_p, y_s = _final_norm(x, final_g)
    out = (y_p.reshape(batch, seq, d), y_s.reshape(dec_batch, dec_seq, d))
    for st in layer_state:
        out = out + tuple(st)
    return out
```

```python
import functools
import math

import jax
import jax.numpy as jnp
from jax import lax
from jax.experimental import pallas as pl
from jax.experimental.pallas import tpu as pltpu

F32 = jnp.float32
BF16 = jnp.bfloat16

NORM_EPS = 1e-6
VMEM_LIMIT_BYTES = 56 * 1024 * 1024
ROW_TILE = 512
LANES = 128
SUBLANES = 8


def _params(*semantics):
    return pltpu.CompilerParams(dimension_semantics=semantics, vmem_limit_bytes=VMEM_LIMIT_BYTES)


def _const_spec(shape):
    return pl.BlockSpec(shape, lambda *_: (0,) * len(shape))


def _rmsnorm(x, g):
    ms = jnp.mean(x * x, axis=-1, keepdims=True)
    return x * lax.rsqrt(ms + NORM_EPS) * g


def _dot(a, b):
    return jnp.dot(a, b, preferred_element_type=F32)


def _dot_nt(a, b):
    return lax.dot_general(a, b, (((1,), (1,)), ((), ())), preferred_element_type=F32)


def _dot_tn(a, b):
    return lax.dot_general(a, b, (((0,), (0,)), ((), ())), preferred_element_type=F32)


FFN_CHUNK = 256


def _ffn_body(rows, consts, outs, is_sample, tile_index):
    (x_ref,), (g_ref, win_ref, wout_ref), (o_ref,) = rows, consts, outs
    d_ff = wout_ref.shape[0]
    x = x_ref[...]
    xn = _rmsnorm(x, g_ref[...]).astype(BF16)
    acc = jnp.zeros_like(x)
    for c in range(d_ff // FFN_CHUNK):
        lo = c * FFN_CHUNK
        gate = _dot(xn, win_ref[:, lo:lo + FFN_CHUNK])
        up = _dot(xn, win_ref[:, d_ff + lo:d_ff + lo + FFN_CHUNK])
        act = (gate * jax.nn.sigmoid(gate) * up).astype(BF16)
        acc = acc + _dot(act, wout_ref[lo:lo + FFN_CHUNK, :])
    o_ref[...] = x + 0.5 * acc


def _ffn(x, g, w_in, w_out):
    d = w_in.shape[0]
    return _row_call(_ffn_body, [x], [g.reshape(1, d), w_in, w_out], [(d, F32, False)], "ffn")[0]


def _row_call(body, rows, consts, outs, name, tile_div=1):
    n_p, n_s = rows[0][0].shape[0], rows[0][1].shape[0]
    tile = ROW_TILE // tile_div
    assert n_p % tile == 0 and n_s % tile == 0
    tp, ts = n_p // tile, n_s // tile
    p_idx = lambda i: jnp.minimum(i, tp - 1)
    s_idx = lambda i: jnp.maximum(i - tp, 0)
    per_tile = tile // SUBLANES
    prev_of = lambda t: jnp.maximum(t * per_tile - 1, 0)
    in_specs, args, row_slots = [], [], []
    for entry in rows:
        prev = len(entry) == 3
        slots = []
        for a, idx in ((entry[0], p_idx), (entry[1], s_idx)):
            if a is None:
                slots.append(None)
                continue
            if prev:
                in_specs.append(pl.BlockSpec((SUBLANES, a.shape[1]), lambda i, idx=idx: (prev_of(idx(i)), 0)))
            else:
                in_specs.append(pl.BlockSpec((tile, a.shape[1]), lambda i, idx=idx: (idx(i), 0)))
            slots.append(len(args))
            args.append(a)
        row_slots.append(slots)
    n_row_args = len(args)
    for c in consts:
        in_specs.append(_const_spec(c.shape))
        args.append(c)
    out_specs, out_shape, out_slots = [], [], []
    for cols, dtype, kind in outs:
        slots = []
        if kind == "whole":
            out_specs.append(_const_spec(cols))
            out_shape.append(jax.ShapeDtypeStruct(cols, dtype))
            slots = [len(out_shape) - 1] * 2
        else:
            for n, idx, absent in ((n_p, p_idx, kind == "sample"), (n_s, s_idx, False)):
                if absent:
                    slots.append(None)
                    continue
                if kind is True:
                    out_specs.append(pl.BlockSpec((cols, tile), lambda i, idx=idx: (0, idx(i))))
                    out_shape.append(jax.ShapeDtypeStruct((cols, n), dtype))
                else:
                    out_specs.append(pl.BlockSpec((tile, cols), lambda i, idx=idx: (idx(i), 0)))
                    out_shape.append(jax.ShapeDtypeStruct((n, cols), dtype))
                slots.append(len(out_shape) - 1)
        out_slots.append(slots)
    nc = len(consts)

    def kern(*refs):
        i = pl.program_id(0)
        r, c, o = refs[:n_row_args], refs[n_row_args:n_row_args + nc], refs[n_row_args + nc:]
        pick = lambda seq, slots, side: [None if s[side] is None else seq[s[side]] for s in slots]

        @pl.when(i < tp)
        def _():
            body(pick(r, row_slots, 0), c, pick(o, out_slots, 0), False, i)

        @pl.when(i >= tp)
        def _():
            body(pick(r, row_slots, 1), c, pick(o, out_slots, 1), True, i - tp)

    res = pl.pallas_call(
        kern, grid=(tp + ts,), in_specs=in_specs, out_specs=out_specs, out_shape=out_shape,
        compiler_params=_params("arbitrary"), name=name,
    )(*args)
    result = []
    for (cols, dtype, kind), slots in zip(outs, out_slots):
        if kind == "whole":
            result.append(res[slots[0]])
        else:
            result.append(tuple(None if s is None else res[s] for s in slots))
    return result


def _split3(x):
    hi = x.astype(BF16)
    r1 = x - hi.astype(F32)
    mid = r1.astype(BF16)
    lo = (r1 - mid.astype(F32)).astype(BF16)
    return hi, mid, lo


def _dot01(m01, x):
    hi, mid, lo = _split3(x)
    return _dot(m01, hi) + _dot(m01, mid) + _dot(m01, lo)


def _dot01_r(x, m01):
    hi, mid, lo = _split3(x)
    return _dot(hi, m01) + _dot(mid, m01) + _dot(lo, m01)


def _log_sigmoid(x):
    return jnp.minimum(x, 0.0) - jnp.log1p(jnp.exp(-jnp.abs(x)))


FOX_HEADS = 16
FOX_HEAD_DIM = 64
FOX_WIDTH = FOX_HEADS * FOX_HEAD_DIM
FOX_SCALE = FOX_HEAD_DIM ** -0.5


def _fox_proj_body(rows, consts, outs, is_sample, tile_index):
    (x_ref,) = rows
    g_ref, w_ref, wf_ref, bf_ref = consts
    q_ref, k_ref, v_ref, gate_ref, f_ref, ft_ref = outs
    xn = _rmsnorm(x_ref[...], g_ref[...]).astype(BF16)
    for j, o_ref in enumerate((q_ref, k_ref, v_ref, gate_ref)):
        o_ref[...] = _dot(xn, w_ref[:, j * FOX_WIDTH:(j + 1) * FOX_WIDTH])
    logf = _log_sigmoid(_dot(xn, wf_ref[...]) + bf_ref[...])
    f_ref[...] = logf[:, :FOX_HEADS]
    ft_ref[...] = logf.T[:FOX_HEADS, :]


def _fox_project(x, g, w_in, b_f):
    w_main = w_in[:, :4 * FOX_WIDTH].astype(BF16)
    w_f = jnp.pad(w_in[:, 4 * FOX_WIDTH:], ((0, 0), (0, LANES - FOX_HEADS))).astype(BF16)
    b = jnp.pad(b_f, (0, LANES - FOX_HEADS)).reshape(1, LANES)
    outs = [(FOX_WIDTH, F32, False)] * 4 + [(FOX_HEADS, F32, False), (FOX_HEADS, F32, True)]
    return _row_call(_fox_proj_body, [x], [g.reshape(1, -1), w_main, w_f, b], outs, "fox_proj")


def _fox_cumsum_kernel(f_ref, ft_ref, d_ref, dt_ref, *, seq):
    r = lax.broadcasted_iota(jnp.int32, (LANES, LANES), 0)
    c = lax.broadcasted_iota(jnp.int32, (LANES, LANES), 1)
    lower = jnp.where(c <= r, 1.0, 0.0).astype(BF16)
    upper = jnp.where(r <= c, 1.0, 0.0).astype(BF16)
    carry_c = jnp.zeros((1, FOX_HEADS), F32)
    carry_r = jnp.zeros((FOX_HEADS, 1), F32)
    for blk in range(seq // LANES):
        sl = slice(blk * LANES, (blk + 1) * LANES)
        loc = _dot01(lower, f_ref[sl, :]) + carry_c
        d_ref[sl, :] = loc
        carry_c = loc[LANES - 1:LANES, :]
        loc_r = _dot01_r(ft_ref[:, sl], upper) + carry_r
        dt_ref[:, sl] = loc_r
        carry_r = loc_r[:, LANES - 1:LANES]


def _fox_cumsum(f, ft, batch, seq):
    return pl.pallas_call(
        functools.partial(_fox_cumsum_kernel, seq=seq),
        grid=(batch,),
        in_specs=[pl.BlockSpec((seq, FOX_HEADS), lambda b: (b, 0)),
                  pl.BlockSpec((FOX_HEADS, seq), lambda b: (0, b))],
        out_specs=[pl.BlockSpec((seq, FOX_HEADS), lambda b: (b, 0)),
                   pl.BlockSpec((FOX_HEADS, seq), lambda b: (0, b))],
        out_shape=[jax.ShapeDtypeStruct(f.shape, F32), jax.ShapeDtypeStruct(ft.shape, F32)],
        compiler_params=_params("parallel"),
        name="fox_cumsum",
    )(f, ft)


def _pair_head_norm(x, g, first_head):
    x2 = x * x
    s0 = jnp.sum(jnp.where(first_head, x2, 0.0), axis=-1, keepdims=True)
    s1 = jnp.sum(jnp.where(first_head, 0.0, x2), axis=-1, keepdims=True)
    inv = jnp.where(first_head, lax.rsqrt(s0 / FOX_HEAD_DIM + NORM_EPS), lax.rsqrt(s1 / FOX_HEAD_DIM + NORM_EPS))
    return x * inv * g


FOX_Q_BLOCK = 512


FOX_PAGE_GROUP = 16
FOX_PAGE_SUBGROUP = 4


def _fox_sample_kernel(pt_ref, q_ref, k_ref, v_ref, f_ref, gq_ref, gk_ref, *rest, group, n_groups, t_new):
    kp, vp, fp = rest[:group], rest[group:2 * group], rest[2 * group:3 * group]
    o_ref, kn_ref = rest[3 * group:3 * group + 2]
    q_sc, kn_sc, cq_sc, ck_sc, s_sc, m_sc, l_sc, acc_sc, carry_sc = rest[3 * group + 2:]
    step = pl.program_id(1)
    hd, rows_h = FOX_HEAD_DIM, SUBLANES
    rows, width = q_sc.shape
    page = kp[0].shape[3]
    row_head = lax.broadcasted_iota(jnp.int32, (rows, 1), 0) // rows_h

    @pl.when(step == 0)
    def _():
        own_cols = row_head == lax.broadcasted_iota(jnp.int32, (1, width), 1) // hd
        own_head = row_head == lax.broadcasted_iota(jnp.int32, (1, FOX_HEADS), 1)
        pad_rows = lambda x: jnp.concatenate([x, jnp.zeros((rows_h - t_new, x.shape[1]), F32)], axis=0)
        per_head = lambda x8: jnp.broadcast_to(x8[None], (FOX_HEADS,) + x8.shape).reshape(rows, x8.shape[1])

        def head_norm(x, g):
            xe = jnp.where(own_cols, per_head(pad_rows(x)), 0.0)
            ss = jnp.sum(xe * xe, axis=-1, keepdims=True)
            return xe * lax.rsqrt(ss / hd + NORM_EPS) * g

        q_sc[...] = head_norm(q_ref[0], gq_ref[...]) * FOX_SCALE
        kn = jnp.sum(head_norm(k_ref[0], gk_ref[...]).reshape(FOX_HEADS, rows_h, width), axis=0)
        kn_sc[...] = kn
        kn_ref[0] = kn[:t_new, :]
        f = f_ref[0]
        c_rows = [f[0:1, :]]
        for t in range(1, t_new):
            c_rows.append(c_rows[-1] + f[t:t + 1, :])
        c = jnp.concatenate(c_rows, axis=0)
        cq_sc[...] = jnp.sum(jnp.where(own_head, per_head(pad_rows(c)), 0.0), axis=-1, keepdims=True)
        for j in range(t_new):
            ck_sc[j] = jnp.sum(jnp.where(own_head, c[j:j + 1, :], 0.0), axis=-1, keepdims=True)
        m_sc[...] = jnp.full(m_sc.shape, -jnp.inf, F32)
        l_sc[...] = jnp.zeros(l_sc.shape, F32)
        acc_sc[...] = jnp.zeros(acc_sc.shape, F32)
        carry_sc[...] = jnp.zeros(carry_sc.shape, F32)

    r = lax.broadcasted_iota(jnp.int32, (page, page), 0)
    c_ = lax.broadcasted_iota(jnp.int32, (page, page), 1)
    later = jnp.where(r > c_, 1.0, 0.0).astype(BF16)
    qb = q_sc[...].astype(BF16)
    cq = cq_sc[...]
    carry = carry_sc[...]
    for g in reversed(range(group)):
        ft = fp[g][0]
        suffix = _dot01_r(ft, later) + carry
        carry = carry + jnp.sum(ft, axis=-1, keepdims=True)
        bias = jnp.broadcast_to(suffix[:, None, :], (FOX_HEADS, rows_h, page)).reshape(rows, page)
        kb = kp[g][0].reshape(width, page).astype(BF16)
        s_sc[:, g * page:(g + 1) * page] = _dot(qb, kb) + bias + cq
    carry_sc[...] = carry

    m_run, l_run = m_sc[...], l_sc[...]
    acc = acc_sc[...]
    for first in range(0, group, FOX_PAGE_SUBGROUP):
        pages = range(first, min(first + FOX_PAGE_SUBGROUP, group))
        s_sub = s_sc[:, pages[0] * page:(pages[-1] + 1) * page]
        m_new = jnp.maximum(m_run, jnp.max(s_sub, axis=-1, keepdims=True))
        alpha = jnp.exp(m_run - m_new)
        p_sub = jnp.exp(s_sub - m_new)
        l_run = alpha * l_run + jnp.sum(p_sub, axis=-1, keepdims=True)
        m_run = m_new
        vt = jnp.concatenate([vp[g][0].reshape(width, page).astype(BF16) for g in pages], axis=1)
        acc = alpha * acc + _dot_nt(p_sub.astype(BF16), vt)
    m_sc[...] = m_run
    l_sc[...] = l_run
    acc_sc[...] = acc

    @pl.when(step == n_groups - 1)
    def _():
        t_row = lax.broadcasted_iota(jnp.int32, (rows, 1), 0) % rows_h
        own_cols = row_head == lax.broadcasted_iota(jnp.int32, (1, width), 1) // hd
        q_all = q_sc[...]
        s_new = []
        for j in range(t_new):
            s_j = jnp.sum(q_all * kn_sc[j:j + 1, :], axis=-1, keepdims=True) + cq - ck_sc[j]
            s_new.append(jnp.where(t_row >= j, s_j, -jnp.inf))
        m_new = m_run
        for s_j in s_new:
            m_new = jnp.maximum(m_new, s_j)
        alpha = jnp.exp(m_run - m_new)
        l = alpha * l_run
        o = alpha * acc
        v = v_ref[0]
        for j, s_j in enumerate(s_new):
            p_j = jnp.exp(s_j - m_new)
            l = l + p_j
            o = o + p_j * v[j:j + 1, :]
        o = jnp.where(own_cols, o / l, 0.0)
        o_ref[0] = jnp.sum(o.reshape(FOX_HEADS, rows_h, width), axis=0)[:t_new, :]


def _fox_sample_attention(q, k, v, f, cache_kt, cache_vt, cache_ft, page_table, gq, gk):
    batch, t_new, width = q.shape
    n_pages = page_table.shape[1]
    page = cache_kt.shape[3]
    group = FOX_PAGE_GROUP
    n_groups = n_pages // group
    rows = FOX_HEADS * SUBLANES

    def page_spec(shape, g):
        def index(b, s, pt):
            return (pt[b * n_pages + (n_groups - 1 - s) * group + g],) + (0,) * (len(shape) - 1)
        return pl.BlockSpec((1,) + shape[1:], index)

    tok_spec = lambda last: pl.BlockSpec((1, t_new, last), lambda b, s, pt: (b, 0, 0))
    in_specs = [tok_spec(width), tok_spec(width), tok_spec(width), tok_spec(FOX_HEADS),
                pl.BlockSpec((1, width), lambda b, s, pt: (0, 0)),
                pl.BlockSpec((1, width), lambda b, s, pt: (0, 0))]
    in_specs += [page_spec(cache_kt.shape, g) for g in range(group)]
    in_specs += [page_spec(cache_vt.shape, g) for g in range(group)]
    in_specs += [page_spec(cache_ft.shape, g) for g in range(group)]
    grid_spec = pltpu.PrefetchScalarGridSpec(
        num_scalar_prefetch=1,
        grid=(batch, n_groups),
        in_specs=in_specs,
        out_specs=[tok_spec(width), tok_spec(width)],
        scratch_shapes=[
            pltpu.VMEM((rows, width), F32),
            pltpu.VMEM((SUBLANES, width), F32),
            pltpu.VMEM((rows, 1), F32),
            pltpu.VMEM((t_new, rows, 1), F32),
            pltpu.VMEM((rows, group * page), F32),
            pltpu.VMEM((rows, 1), F32),
            pltpu.VMEM((rows, 1), F32),
            pltpu.VMEM((rows, width), F32),
            pltpu.VMEM((FOX_HEADS, 1), F32),
        ],
    )
    heads_tiled = lambda g: jnp.tile(g, FOX_HEADS).reshape(1, width)
    return pl.pallas_call(
        functools.partial(_fox_sample_kernel, group=group, n_groups=n_groups, t_new=t_new),
        grid_spec=grid_spec,
        out_shape=[jax.ShapeDtypeStruct(q.shape, F32), jax.ShapeDtypeStruct(q.shape, F32)],
        compiler_params=_params("parallel", "arbitrary"),
        name="fox_sample_attention",
    )(page_table.reshape(-1), q, k, v, f, heads_tiled(gq), heads_tiled(gk),
      *([cache_kt] * group), *([cache_vt] * group), *([cache_ft] * group))


LOG2E = 1.4426950408889634
FOX_BIAS_ROWS = 16


def _fox_prompt_kernel(q_ref, k_ref, v_ref, d_ref, dt_ref, gq_ref, gk_ref, o_ref, knt_ref, vt_ref, kt_sc, v_sc, *, tq):
    pair = pl.program_id(1)
    qi = pl.program_id(2)
    first_head = lax.broadcasted_iota(jnp.int32, (1, LANES), 1) < FOX_HEAD_DIM
    n_blk = kt_sc.shape[0]

    @pl.when(qi == 0)
    def _():
        knt = _pair_head_norm(k_ref[...], gk_ref[...], first_head).T
        seq = knt.shape[1]
        knt_ref[0] = knt.reshape(2, FOX_HEAD_DIM, seq)
        sub = lax.broadcasted_iota(jnp.int32, (FOX_BIAS_ROWS, 1), 0)
        bias_rows = jnp.where(sub < 3, 1.0, 0.0).astype(BF16) * jnp.ones((1, seq), BF16)
        for e in (0, 1):
            parts = _split3(dt_ref[pl.ds(2 * pair + e, 1), :] * (-LOG2E))
            for i, part in enumerate(parts):
                bias_rows = jnp.where(sub == 3 + 3 * e + i, part, bias_rows)
        for j in range(n_blk):
            cols = slice(j * tq, (j + 1) * tq)
            kt_sc[j, :LANES, :] = knt[:, cols].astype(BF16)
            kt_sc[j, LANES:, :] = bias_rows[:, cols]
        v = v_ref[...]
        v_sc[...] = v.astype(BF16)
        vt = v.T
        vt_ref[0] = vt.reshape(2, FOX_HEAD_DIM, seq)

    qn = _pair_head_norm(q_ref[...], gq_ref[...], first_head) * (FOX_SCALE * LOG2E)
    d_blk = d_ref[...]
    head_lane = lax.broadcasted_iota(jnp.int32, (1, FOX_HEADS), 1)
    lane_b = lax.broadcasted_iota(jnp.int32, (1, FOX_BIAS_ROWS), 1)
    q_heads = []
    for e in (0, 1):
        d_q = jnp.sum(jnp.where(head_lane == 2 * pair + e, d_blk, 0.0), axis=-1, keepdims=True) * LOG2E
        ones_at = (lane_b >= 3 + 3 * e) & (lane_b < 6 + 3 * e)
        bias_cols = jnp.where(ones_at, 1.0, 0.0).astype(BF16) * jnp.ones((tq, 1), BF16)
        for i, part in enumerate(_split3(d_q)):
            bias_cols = jnp.where(lane_b == i, part, bias_cols)
        mine = first_head if e == 0 else jnp.logical_not(first_head)
        q_heads.append(jnp.concatenate([jnp.where(mine, qn, 0.0).astype(BF16), bias_cols], axis=1))
    row = lax.broadcasted_iota(jnp.int32, (tq, tq), 0)
    col = lax.broadcasted_iota(jnp.int32, (tq, tq), 1)

    def attend(n_kv):
        scores = lambda j: [_dot(q_heads[e], kt_sc[j]) for e in (0, 1)]
        carry = [(jnp.full((tq, 1), -jnp.inf, F32), jnp.zeros((tq, 1), F32), jnp.zeros((tq, LANES), F32))
                 for _ in (0, 1)]
        s_cur = scores(0)
        for j in range(n_kv):
            s_next = scores(j + 1) if j + 1 < n_kv else None
            vb = v_sc[j * tq:(j + 1) * tq, :]
            for e in (0, 1):
                m, l, acc = carry[e]
                s = s_cur[e]
                if j == n_kv - 1:
                    s = jnp.where(col <= row, s, -jnp.inf)
                m_new = jnp.maximum(m, jnp.max(s, axis=-1, keepdims=True))
                alpha = jnp.exp2(m - m_new)
                p = jnp.exp2(s - m_new)
                l = alpha * l + jnp.sum(p, axis=-1, keepdims=True)
                acc = alpha * acc + _dot(p.astype(BF16), vb)
                carry[e] = (m_new, l, acc)
            s_cur = s_next
        (_, l0, acc0), (_, l1, acc1) = carry
        return jnp.where(first_head, acc0 / l0, acc1 / l1)

    for n_kv in range(1, n_blk + 1):
        @pl.when(qi == n_kv - 1)
        def _(n_kv=n_kv):
            o_ref[...] = attend(n_kv)


def _fox_prompt_attention(q, k, v, dcum, dcum_t, gq, gk, batch, seq):
    tq = FOX_Q_BLOCK
    nq = seq // tq
    n = batch * seq
    pairs = FOX_WIDTH // LANES
    gq2 = jnp.tile(gq, 2).reshape(1, LANES)
    gk2 = jnp.tile(gk, 2).reshape(1, LANES)
    t_shape = jax.ShapeDtypeStruct((batch, FOX_HEADS, FOX_HEAD_DIM, seq), F32)
    t_spec = pl.BlockSpec((1, 2, FOX_HEAD_DIM, seq), lambda b, p, i: (b, p, 0, 0))
    return pl.pallas_call(
        functools.partial(_fox_prompt_kernel, tq=tq),
        grid=(batch, pairs, nq),
        in_specs=[
            pl.BlockSpec((tq, LANES), lambda b, p, i: (b * nq + i, p)),
            pl.BlockSpec((seq, LANES), lambda b, p, i: (b, p)),
            pl.BlockSpec((seq, LANES), lambda b, p, i: (b, p)),
            pl.BlockSpec((tq, FOX_HEADS), lambda b, p, i: (b * nq + i, 0)),
            pl.BlockSpec((FOX_HEADS, seq), lambda b, p, i: (0, b)),
            _const_spec((1, LANES)),
            _const_spec((1, LANES)),
        ],
        out_specs=[pl.BlockSpec((tq, LANES), lambda b, p, i: (b * nq + i, p)), t_spec, t_spec],
        out_shape=[jax.ShapeDtypeStruct((n, FOX_WIDTH), F32), t_shape, t_shape],
        scratch_shapes=[pltpu.VMEM((nq, LANES + FOX_BIAS_ROWS, tq), BF16), pltpu.VMEM((seq, LANES), BF16)],
        compiler_params=_params("parallel", "parallel", "arbitrary"),
        name="fox_prompt_attention",
    )(q, k, v, dcum, dcum_t, gq2, gk2)


def _fox_out_body(rows, consts, outs, is_sample, tile_index):
    (x_ref, o_ref, gate_ref), (w_ref,), (y_ref,) = rows, consts, outs
    gated = (o_ref[...] * jax.nn.sigmoid(gate_ref[...])).astype(BF16)
    y_ref[...] = x_ref[...] + _dot(gated, w_ref[...])


def _fox_layer(x, g, w_in, b_f, gq, gk, w_out, cache_k, cache_v, cache_f, page_table, dims):
    batch, seq, dec_batch, dec_seq = dims
    q, k, v, gate, f, ft = _fox_project(x, g, w_in, b_f)
    dcum, dcum_t = _fox_cumsum(f[0], ft[0], batch, seq)
    o_p, knt_p, vt_p = _fox_prompt_attention(q[0], k[0], v[0], dcum, dcum_t, gq, gk, batch, seq)
    per_seq = lambda a: a.reshape(dec_batch, dec_seq, a.shape[-1])
    o_s, kn_s = _fox_sample_attention(per_seq(q[1]), per_seq(k[1]), per_seq(v[1]), per_seq(f[1]),
                                      jnp.transpose(cache_k, (0, 2, 3, 1)), jnp.transpose(cache_v, (0, 2, 3, 1)),
                                      jnp.swapaxes(cache_f, 1, 2), page_table, gq, gk)
    o = (o_p, o_s.reshape(-1, FOX_WIDTH))
    (y,) = _row_call(_fox_out_body, [x, o, gate], [w_out.astype(BF16)], [(w_out.shape[1], F32, False)], "fox_out")
    heads = lambda a, b, t: a.reshape(b, t, FOX_HEADS, FOX_HEAD_DIM)
    seq_first = lambda a: jnp.transpose(a, (0, 3, 1, 2))
    state = (seq_first(knt_p), seq_first(vt_p), f[0].reshape(batch, seq, FOX_HEADS),
             heads(kn_s, dec_batch, dec_seq), heads(v[1], dec_batch, dec_seq),
             f[1].reshape(dec_batch, dec_seq, FOX_HEADS))
    return y, state


RWKV_HEAD_DIM = 64
RWKV_GN_EPS = 64e-5
RWKV_GATE_LORA_PAD = 256
RWKV_CHUNK = 64
RWKV_CHUNKS_PER_STEP = 8


def _rwkv_proj_body(rows, consts, outs, is_sample, tile_index, *, seq, dec_seq):
    x_ref, prev_ref, start_ref = rows
    (g_ref, mu_ref, w_rkv_ref, w0_ref, w1_ref, w2_ref, a0_ref, a1_ref, a2_ref, g1_ref, g2_ref) = consts
    r_ref, k_ref, v_ref, ld_ref, a_ref, gate_ref, xn_ref, last_ref = outs
    tile = x_ref.shape[0]
    xn = _rmsnorm(x_ref[...], g_ref[...])
    rolled = pltpu.roll(xn, 1, 0)
    row = lax.broadcasted_iota(jnp.int32, (tile, 1), 0)
    if is_sample:
        xn_ref[...] = xn
        x_prev = jnp.where(row % dec_seq == 0, start_ref[...], rolled)
    else:
        before = _rmsnorm(prev_ref[SUBLANES - 1:SUBLANES, :], g_ref[...])
        before = jnp.where((tile_index * tile) % seq == 0, 0.0, before)
        x_prev = jnp.where(row == 0, before, rolled)

        @pl.when(((tile_index + 1) * tile) % seq == 0)
        def _():
            last_ref[pl.ds((tile_index * tile) // seq, 1), :] = xn[tile - 1:tile, :]

    xx = x_prev - xn
    mix = lambda i: (xn + xx * mu_ref[i:i + 1, :]).astype(BF16)
    r_ref[...] = _dot(mix(0), w_rkv_ref[0])
    k_ref[...] = _dot(mix(2), w_rkv_ref[1])
    v_ref[...] = _dot(mix(3), w_rkv_ref[2])
    wl = w0_ref[...] + _dot(jnp.tanh(_dot(mix(1), w1_ref[...])).astype(BF16), w2_ref[...])
    softplus_neg = jnp.maximum(-wl, 0.0) + jnp.log1p(jnp.exp(-jnp.abs(wl)))
    ld_ref[...] = -jnp.exp(-softplus_neg - 0.5)
    a_ref[...] = jax.nn.sigmoid(a0_ref[...] + _dot(_dot(mix(4), a1_ref[...]).astype(BF16), a2_ref[...]))
    gate_ref[...] = _dot(jax.nn.sigmoid(_dot(mix(5), g1_ref[...])).astype(BF16), g2_ref[...])


def _dot3(a, b):
    a_hi = a.astype(BF16)
    a_lo = (a - a_hi.astype(F32)).astype(BF16)
    b_hi = b.astype(BF16)
    b_lo = (b - b_hi.astype(F32)).astype(BF16)
    return _dot(a_hi, b_hi) + _dot(a_hi, b_lo) + _dot(a_lo, b_hi)


def _pair_sums(x, first_head):
    s0 = jnp.sum(jnp.where(first_head, x, 0.0), axis=-1, keepdims=True)
    s1 = jnp.sum(jnp.where(first_head, 0.0, x), axis=-1, keepdims=True)
    return jnp.where(first_head, s0, s1)


def _rwkv_chunk_terms(chunks, kk_w, ka_w):
    length = chunks[0][0].shape[0]
    first_head = lax.broadcasted_iota(jnp.int32, (1, LANES), 1) < RWKV_HEAD_DIM
    head_masks = (first_head, jnp.logical_not(first_head))
    ti = lax.broadcasted_iota(jnp.int32, (length, length), 0)
    si = lax.broadcasted_iota(jnp.int32, (length, length), 1)
    strict, incl = si < ti, si <= ti
    lower01 = jnp.where(incl, 1.0, 0.0).astype(BF16)
    levels = max(1, math.ceil(math.log2(length)))
    insts = [(c, e) for c in range(len(chunks)) for e in (0, 1)]

    cl = [_dot01(lower01, ld) for (_, _, _, ld, _) in chunks]
    pre = []
    for (r, k, v, ld, a), cl_c in zip(chunks, cl):
        kk = k * kk_w
        kkn = kk / jnp.maximum(jnp.sqrt(_pair_sums(kk * kk, first_head)), 1e-12)
        k2 = k * (1.0 + (a - 1.0) * ka_w)
        aa, bb = -kkn, kkn * a
        cl_last = cl_c[length - 1:length, :]
        p_out = jnp.exp(-cl_c)
        p_end = jnp.exp(cl_last - cl_c)
        pre.append(dict(
            k2=k2, vb=v.astype(BF16), r_t=r * jnp.exp(cl_c), a_t=aa * jnp.exp(cl_c - ld),
            rhs=jnp.concatenate([bb * p_out, k2 * p_out], axis=0).astype(BF16),
            b_end=(bb * p_end).astype(BF16), k_end=(k2 * p_end).astype(BF16), decay_end=jnp.exp(cl_last)))
    a_e = [jnp.where(head_masks[e], pre[c]["a_t"], 0.0) for c, e in insts]
    r_e = [jnp.where(head_masks[e], pre[c]["r_t"], 0.0) for c, e in insts]
    mm = [_dot_nt(jnp.concatenate([a_e[i], r_e[i]], axis=0).astype(BF16), pre[c]["rhs"])
          for i, (c, e) in enumerate(insts)]
    m_ak = [jnp.where(strict, m[:length, length:], 0.0).astype(BF16) for m in mm]
    m_rb = [jnp.where(incl, m[length:, :length], 0.0).astype(BF16) for m in mm]
    m_rk = [jnp.where(incl, m[length:, length:], 0.0).astype(BF16) for m in mm]
    power = [jnp.where(strict, m[:length, :length], 0.0) for m in mm]
    x = [jnp.concatenate([a_e[i], _dot(m_ak[i], pre[c]["vb"])], axis=1) for i, (c, e) in enumerate(insts)]
    for lvl in range(levels):
        pb = [p_i.astype(BF16) for p_i in power]
        x = [x_i + _dot(p_i, x_i.astype(BF16)) for x_i, p_i in zip(x, pb)]
        if lvl + 1 < levels:
            power = [_dot(p_i, p_i) for p_i in pb]
    w_e = [x_i[:, :LANES] for x_i in x]
    u0_e = [x_i[:, LANES:] for x_i in x]
    yr_e = [r_e[i] + _dot(m_rb[i], w_e[i].astype(BF16)) for i in range(len(insts))]
    y0_e = [_dot(m_rb[i], u0_e[i].astype(BF16)) + _dot(m_rk[i], pre[c]["vb"]) for i, (c, e) in enumerate(insts)]
    row = lax.broadcasted_iota(jnp.int32, (LANES, LANES), 0)
    col = lax.broadcasted_iota(jnp.int32, (LANES, LANES), 1)
    same_head = (row < RWKV_HEAD_DIM) == (col < RWKV_HEAD_DIM)
    out = []
    for c in range(len(chunks)):
        i0, i1 = 2 * c, 2 * c + 1
        w = (w_e[i0] + w_e[i1]).astype(BF16)
        u0 = jnp.where(first_head, u0_e[i0], u0_e[i1]).astype(BF16)
        g = jnp.where(same_head, jnp.where(row == col, pre[c]["decay_end"], 0.0) + _dot_tn(pre[c]["b_end"], w), 0.0)
        h = jnp.where(same_head, _dot_tn(pre[c]["b_end"], u0) + _dot_tn(pre[c]["k_end"], pre[c]["vb"]), 0.0)
        out.append((g, h, yr_e[i0] + yr_e[i1], jnp.where(first_head, y0_e[i0], y0_e[i1]), pre[c]["k2"]))
    return out


def _rwkv_finish(y, r, k2, v, gate, rk_w, gnw, gnb):
    first_head = lax.broadcasted_iota(jnp.int32, (1, LANES), 1) < RWKV_HEAD_DIM
    mean = _pair_sums(y, first_head) / RWKV_HEAD_DIM
    yc = y - mean
    var = _pair_sums(yc * yc, first_head) / RWKV_HEAD_DIM
    yn = yc * lax.rsqrt(var + RWKV_GN_EPS) * gnw + gnb
    bonus = _pair_sums(r * k2 * rk_w, first_head) * v
    return (yn + bonus) * gate


def _rwkv_scan_kernel(r_ref, k_ref, v_ref, ld_ref, a_ref, gate_ref, kk_ref, ka_ref, rk_ref, gnw_ref, gnb_ref,
                      z_ref, s_ref, state_sc, *, length, n_inst):
    step = pl.program_id(2)

    @pl.when(step == 0)
    def _():
        state_sc[...] = jnp.zeros(state_sc.shape, F32)

    state = state_sc[...]
    rows = [slice(i * length, (i + 1) * length) for i in range(n_inst)]
    chunks = [(r_ref[rs, :], k_ref[rs, :], v_ref[rs, :], ld_ref[rs, :], a_ref[rs, :]) for rs in rows]
    terms = _rwkv_chunk_terms(chunks, kk_ref[...], ka_ref[...])
    for rs, (r, k, v, ld, a), (g, h, yr, y0, k2) in zip(rows, chunks, terms):
        y = _dot(yr.astype(BF16), state.astype(BF16)) + y0
        state = _dot3(g, state) + h
        z_ref[rs, :] = _rwkv_finish(y, r, k2, v, gate_ref[rs, :], rk_ref[...], gnw_ref[...], gnb_ref[...])
    state_sc[...] = state

    @pl.when(step == pl.num_programs(2) - 1)
    def _():
        st = state.T
        s_ref[0, 0] = st[:RWKV_HEAD_DIM, :RWKV_HEAD_DIM]
        s_ref[0, 1] = st[RWKV_HEAD_DIM:, RWKV_HEAD_DIM:]


def _rwkv_scan(r, k, v, ld, a, gate, params, n_seq, seq):
    d = r.shape[1]
    pairs = d // LANES
    heads = d // RWKV_HEAD_DIM
    rows = RWKV_CHUNKS_PER_STEP * RWKV_CHUNK
    steps = seq // rows
    row_spec = pl.BlockSpec((rows, LANES), lambda b, p, s: (b * steps + s, p))
    par_spec = pl.BlockSpec((1, LANES), lambda b, p, s: (0, p))
    st_spec = pl.BlockSpec((1, 2, RWKV_HEAD_DIM, RWKV_HEAD_DIM), lambda b, p, s: (b, p, 0, 0))
    return pl.pallas_call(
        functools.partial(_rwkv_scan_kernel, length=RWKV_CHUNK, n_inst=RWKV_CHUNKS_PER_STEP),
        grid=(n_seq, pairs, steps),
        in_specs=[row_spec] * 6 + [par_spec] * 5,
        out_specs=[row_spec, st_spec],
        out_shape=[jax.ShapeDtypeStruct(r.shape, F32),
                   jax.ShapeDtypeStruct((n_seq, heads, RWKV_HEAD_DIM, RWKV_HEAD_DIM), F32)],
        scratch_shapes=[pltpu.VMEM((LANES, LANES), F32)],
        compiler_params=_params("parallel", "parallel", "arbitrary"),
        name="rwkv_scan_prompt",
    )(r, k, v, ld, a, gate, *params)


def _rwkv_sample_kernel(r_ref, k_ref, v_ref, ld_ref, a_ref, gate_ref, kk_ref, ka_ref, rk_ref, gnw_ref, gnb_ref,
                        s0_ref, z_ref, s_ref, y_sc, *, steps):
    n = RWKV_HEAD_DIM
    s_ref[...] = s0_ref[...]
    kk_w, ka_w, rk_w, gnw, gnb = kk_ref[0], ka_ref[0], rk_ref[0], gnw_ref[0], gnb_ref[0]
    for t in range(steps):
        r, k, v, a = r_ref[t, 0], k_ref[t, 0], v_ref[t, 0], a_ref[t, 0]
        decay = jnp.exp(ld_ref[t, 0])
        kk = k * kk_w
        kkn = kk / jnp.maximum(jnp.sqrt(jnp.sum(kk * kk, axis=0, keepdims=True)), 1e-12)
        k2 = k * (1.0 + (a - 1.0) * ka_w)
        aa, bb = -kkn, kkn * a

        def value_row(i, carry):
            s_i = s_ref[0, i]
            sa = jnp.sum(s_i * aa, axis=0, keepdims=True)
            v_i = v_ref[t, 0, pl.ds(i, 1), :]
            s_i = s_i * decay + sa * bb + v_i * k2
            s_ref[0, i] = s_i
            y_sc[pl.ds(i, 1), :] = jnp.sum(s_i * r, axis=0, keepdims=True)
            return carry

        lax.fori_loop(0, n, value_row, 0, unroll=4)
        y = y_sc[...]
        mean = jnp.mean(y, axis=0, keepdims=True)
        yc = y - mean
        var = jnp.mean(yc * yc, axis=0, keepdims=True)
        yn = yc * lax.rsqrt(var + RWKV_GN_EPS) * gnw + gnb
        bonus = jnp.sum(r * k2 * rk_w, axis=0, keepdims=True) * v
        z_ref[t, 0] = (yn + bonus) * gate_ref[t, 0]


def _rwkv_sample(r, k, v, ld, a, gate, s0, params, dec_batch, dec_seq):
    d = r.shape[1]
    n = RWKV_HEAD_DIM
    heads = d // n
    lanes_last = lambda t: jnp.transpose(t.reshape(dec_batch, dec_seq, heads, n), (1, 2, 3, 0))
    tok_spec = pl.BlockSpec((dec_seq, 1, n, dec_batch), lambda h: (0, h, 0, 0))
    par_spec = pl.BlockSpec((1, n, 1), lambda h: (h, 0, 0))
    st_spec = pl.BlockSpec((1, n, n, dec_batch), lambda h: (h, 0, 0, 0))
    z, s = pl.pallas_call(
        functools.partial(_rwkv_sample_kernel, steps=dec_seq),
        grid=(heads,),
        in_specs=[tok_spec] * 6 + [par_spec] * 5 + [st_spec],
        out_specs=[tok_spec, st_spec],
        out_shape=[jax.ShapeDtypeStruct((dec_seq, heads, n, dec_batch), F32),
                   jax.ShapeDtypeStruct((heads, n, n, dec_batch), F32)],
        scratch_shapes=[pltpu.VMEM((n, dec_batch), F32)],
        compiler_params=_params("parallel"),
        name="rwkv_scan_sample",
    )(*[lanes_last(t) for t in (r, k, v, ld, a, gate)], *[p.reshape(heads, n, 1) for p in params],
      jnp.transpose(s0, (1, 2, 3, 0)))
    return jnp.transpose(z, (3, 0, 1, 2)).reshape(dec_batch * dec_seq, d), jnp.transpose(s, (3, 0, 1, 2))


def _rwkv_out_body(rows, consts, outs, is_sample, tile_index):
    (x_ref, z_ref), (w_ref,), (y_ref,) = rows, consts, outs
    y_ref[...] = x_ref[...] + _dot(z_ref[...].astype(BF16), w_ref[...])


def _rwkv_layer(x, g, shift_state, wkv_state, mu, w0, w1, w2, a0, a1, a2, g1, g2, kk_w, ka_w, rk_w, w_rkv, w_o,
                gn_w, gn_b, dims):
    batch, seq, dec_batch, dec_seq = dims
    d = mu.shape[1]
    heads = d // RWKV_HEAD_DIM
    row = lambda t: t.reshape(1, d)
    starts = jnp.concatenate([shift_state[:, None, :], jnp.zeros((dec_batch, dec_seq - 1, d), F32)], axis=1)
    rank = g1.shape[1]
    g1p = jnp.pad(g1, ((0, 0), (0, RWKV_GATE_LORA_PAD - rank))).astype(BF16)
    g2p = jnp.pad(g2, ((0, RWKV_GATE_LORA_PAD - rank), (0, 0))).astype(BF16)
    consts = [row(g), mu, w_rkv.astype(BF16), row(w0), w1.astype(BF16), w2.astype(BF16), row(a0), a1.astype(BF16),
              a2.astype(BF16), g1p, g2p]
    outs = [(d, F32, False)] * 6 + [(d, F32, "sample"), ((batch, d), F32, "whole")]
    r, k, v, ld, a, gate, xn, last_p = _row_call(
        functools.partial(_rwkv_proj_body, seq=seq, dec_seq=dec_seq),
        [x, (x[0], None, "prev"), (None, starts.reshape(-1, d))], consts, outs, "rwkv_proj", tile_div=2)
    params = [row(t) for t in (kk_w, ka_w, rk_w.reshape(-1), gn_w, gn_b)]
    z_p, s_p = _rwkv_scan(r[0], k[0], v[0], ld[0], a[0], gate[0], params, batch, seq)
    z_s, s_s = _rwkv_sample(r[1], k[1], v[1], ld[1], a[1], gate[1], wkv_state, params, dec_batch, dec_seq)
    (y,) = _row_call(_rwkv_out_body, [x, (z_p, z_s)], [w_o.astype(BF16)], [(d, F32, False)], "rwkv_out")
    shift_s = xn[1].reshape(dec_batch, dec_seq, d)[:, dec_seq - 1]
    return y, (last_p, s_p, shift_s, s_s)


MLSTM_HEADS = 4
MLSTM_QK_DIM = 128
MLSTM_V_DIM = 256
MLSTM_CHUNK = 64
MLSTM_GATE_CAP = 15.0
MLSTM_CHUNKS_PER_STEP = 8
MLSTM_SEQS_PER_STEP = 16
MLSTM_NO_INPUT = -1e30


def _mlstm_proj_body(rows, consts, outs, is_sample, tile_index):
    (x_ref,), (g_ref, w_ref, wg_ref, bg_ref) = rows, consts
    q_ref, k_ref, v_ref, og_ref, ig_ref, igt_ref = outs
    hk = MLSTM_HEADS * MLSTM_QK_DIM
    hv = MLSTM_HEADS * MLSTM_V_DIM
    xn = _rmsnorm(x_ref[...], g_ref[...]).astype(BF16)
    q_ref[...] = _dot(xn, w_ref[:, :hk]) * MLSTM_QK_DIM ** -0.5
    k_ref[...] = _dot(xn, w_ref[:, hk:2 * hk])
    v_ref[...] = _dot(xn, w_ref[:, 2 * hk:2 * hk + hv])
    og_ref[...] = _dot(xn, w_ref[:, 2 * hk + hv:])
    gates = MLSTM_GATE_CAP * jnp.tanh((_dot(xn, wg_ref[...]) + bg_ref[...]) / MLSTM_GATE_CAP)
    is_forget = lax.broadcasted_iota(jnp.int32, (1, LANES), 1) >= MLSTM_HEADS
    ig = jnp.where(is_forget, _log_sigmoid(gates), gates)
    ig_ref[...] = ig[:, :2 * MLSTM_HEADS]
    igt_ref[...] = ig.T[:2 * MLSTM_HEADS, :]


def _mlstm_chunks(chunks, states, head, gn, chained):
    length = chunks[0][0].shape[0]
    nh = MLSTM_HEADS
    count = len(chunks)
    ti = lax.broadcasted_iota(jnp.int32, (length, length), 0)
    si = lax.broadcasted_iota(jnp.int32, (length, length), 1)
    causal = si <= ti
    lower01 = jnp.where(causal, 1.0, 0.0).astype(BF16)
    upper01 = jnp.where(ti <= si, 1.0, 0.0).astype(BF16)
    lane_g = lax.broadcasted_iota(jnp.int32, (1, 2 * nh), 1)
    sub_g = lax.broadcasted_iota(jnp.int32, (2 * nh, 1), 0)
    col = lambda x, j: jnp.sum(jnp.where(lane_g == j, x, 0.0), axis=-1, keepdims=True)
    rowv = lambda x, j: jnp.sum(jnp.where(sub_g == j, x, 0.0), axis=0, keepdims=True)
    last = slice(length - 1, length)

    cum_c = [_dot01(lower01, ch[4]) for ch in chunks]
    cum_r = [_dot01_r(ch[5], upper01) for ch in chunks]
    b_c = [col(x, nh + head) for x in cum_c]
    li_c = [col(ch[4], head) for ch in chunks]
    logw = [jnp.where(causal, b - rowv(xr, nh + head) + rowv(ch[5], head), -jnp.inf)
            for b, xr, ch in zip(b_c, cum_r, chunks)]
    max_logw = [jnp.max(x, axis=-1, keepdims=True) for x in logw]
    qb = [ch[0].astype(BF16) for ch in chunks]
    kb = [ch[1].astype(BF16) for ch in chunks]
    vb = [ch[2].astype(BF16) for ch in chunks]
    qk = [_dot_nt(a, b) for a, b in zip(qb, kb)]
    m_in, m_t = [], []
    for i in range(count):
        m_prev = states[i][2] if (not chained or i == 0) else m_t[i - 1][last, :]
        m_in.append(m_prev)
        m_t.append(jnp.maximum(m_prev + b_c[i], max_logw[i]))
    inter = [m + b for m, b in zip(m_in, b_c)]
    inter_scale = [jnp.exp(a - b) for a, b in zip(inter, m_t)]
    sw = [jnp.exp(lw - mt) * s for lw, mt, s in zip(logw, m_t, qk)]
    swv = [_dot(s.astype(BF16), v) for s, v in zip(sw, vb)]
    m_new = [mt[last, :] for mt in m_t]
    carry_scale = [jnp.exp(a[last, :] - b) for a, b in zip(inter, m_new)]
    w_s = [jnp.exp(b[last, :] - b + li - mn) for b, li, mn in zip(b_c, li_c, m_new)]
    kv = [_dot_tn((ch[2] * w).astype(BF16), k) for ch, w, k in zip(chunks, w_s, kb)]
    ksum = [jnp.sum(w * ch[1], axis=0, keepdims=True) for ch, w in zip(chunks, w_s)]
    c_in, n_in, after = [], [], []
    for i in range(count):
        c_prev, n_prev = (states[i][0], states[i][1]) if (not chained or i == 0) else after[i - 1][:2]
        c_in.append(c_prev)
        n_in.append(n_prev)
        after.append((carry_scale[i] * c_prev + kv[i], carry_scale[i] * n_prev + ksum[i], m_new[i]))
    outs = []
    for i, ch in enumerate(chunks):
        num = inter_scale[i] * _dot_nt(qb[i], c_in[i].astype(BF16)) + swv[i]
        den = inter_scale[i] * jnp.sum(ch[0] * n_in[i], axis=-1, keepdims=True) + jnp.sum(sw[i], axis=-1, keepdims=True)
        h = num / jnp.maximum(jnp.abs(den), jnp.exp(-m_t[i]))
        h = h * lax.rsqrt(jnp.mean(h * h, axis=-1, keepdims=True) + NORM_EPS)
        outs.append(h * gn * jax.nn.sigmoid(ch[3]))
    return outs, after


def _mlstm_scan_kernel(q_ref, k_ref, v_ref, og_ref, ig_ref, igt_ref, gn_ref, c0_ref, nm0_ref,
                       h_ref, c_ref, nm_ref, c_sc, nm_sc, *, length, n_inst, chained):
    head = pl.program_id(1)
    step = pl.program_id(2)
    rows = [slice(i * length, (i + 1) * length) for i in range(n_inst)]
    chunks = [(q_ref[rs, :], k_ref[rs, :], v_ref[rs, :], og_ref[rs, :], ig_ref[rs, :], igt_ref[:, rs]) for rs in rows]
    pack_nm = lambda n, m: jnp.concatenate([n, jnp.broadcast_to(m, n.shape)], axis=0)
    if chained:
        @pl.when(step == 0)
        def _():
            c_sc[...] = c0_ref[0, 0]
            nm_sc[...] = nm0_ref[0, 0]
        states = [(c_sc[...], nm_sc[0:1, :], nm_sc[1:2, 0:1])]
    else:
        states = [(c0_ref[i, 0], nm0_ref[i, 0, 0:1, :], nm0_ref[i, 0, 1:2, 0:1]) for i in range(n_inst)]
    outs, after = _mlstm_chunks(chunks, states, head, gn_ref[...], chained)
    for rs, h in zip(rows, outs):
        h_ref[rs, :] = h
    if chained:
        c, n, m = after[-1]
        c_sc[...] = c
        nm_sc[...] = pack_nm(n, m)

        @pl.when(step == pl.num_programs(2) - 1)
        def _():
            c_ref[0, 0] = c
            nm_ref[0, 0] = pack_nm(n, m)
    else:
        for i, (c, n, m) in enumerate(after):
            c_ref[i, 0] = c
            nm_ref[i, 0] = pack_nm(n, m)


def _mlstm_scan(q, k, v, og, ig, igt, gn, c0, nm0, n_seq, seq, length, n_inst, chained):
    dk, dv, nh = MLSTM_QK_DIM, MLSTM_V_DIM, MLSTM_HEADS
    rows = n_inst * length
    if chained:
        steps = seq // rows
        grid = (n_seq, nh, steps)
        r_idx = lambda b, h, s: b * steps + s
        sb = 1
    else:
        assert seq == length
        grid = (n_seq // n_inst, nh, 1)
        r_idx = lambda b, h, s: b
        sb = n_inst
    c_spec = pl.BlockSpec((sb, 1, dv, dk), lambda b, h, s: (b, h, 0, 0))
    nm_spec = pl.BlockSpec((sb, 1, 2, dk), lambda b, h, s: (b, h, 0, 0))
    return pl.pallas_call(
        functools.partial(_mlstm_scan_kernel, length=length, n_inst=n_inst, chained=chained),
        grid=grid,
        in_specs=[
            pl.BlockSpec((rows, dk), lambda b, h, s: (r_idx(b, h, s), h)),
            pl.BlockSpec((rows, dk), lambda b, h, s: (r_idx(b, h, s), h)),
            pl.BlockSpec((rows, dv), lambda b, h, s: (r_idx(b, h, s), h)),
            pl.BlockSpec((rows, dv), lambda b, h, s: (r_idx(b, h, s), h)),
            pl.BlockSpec((rows, 2 * nh), lambda b, h, s: (r_idx(b, h, s), 0)),
            pl.BlockSpec((2 * nh, rows), lambda b, h, s: (0, r_idx(b, h, s))),
            pl.BlockSpec((1, dv), lambda b, h, s: (0, h)),
            c_spec, nm_spec,
        ],
        out_specs=[pl.BlockSpec((rows, dv), lambda b, h, s: (r_idx(b, h, s), h)), c_spec, nm_spec],
        out_shape=[jax.ShapeDtypeStruct(v.shape, F32), jax.ShapeDtypeStruct((n_seq, nh, dv, dk), F32),
                   jax.ShapeDtypeStruct((n_seq, nh, 2, dk), F32)],
        scratch_shapes=[pltpu.VMEM((dv, dk), F32), pltpu.VMEM((2, dk), F32)],
        compiler_params=_params("parallel", "parallel", "arbitrary"),
        name="mlstm_scan_prompt" if chained else "mlstm_scan_sample",
    )(q, k, v, og, ig, igt, gn, c0, nm0)


def _mlstm_out_body(rows, consts, outs, is_sample, tile_index):
    (x_ref, h_ref), (w_ref,), (y_ref,) = rows, consts, outs
    y_ref[...] = x_ref[...] + _dot(h_ref[...].astype(BF16), w_ref[...])


def _mlstm_layer(x, g, c_state, n_state, m_state, w_in, b_if, gn, w_out, dims):
    batch, seq, dec_batch, dec_seq = dims
    dk, dv, nh = MLSTM_QK_DIM, MLSTM_V_DIM, MLSTM_HEADS
    d = w_in.shape[0]
    n_main = 2 * nh * dk + 2 * nh * dv
    w_main = w_in[:, :n_main].astype(BF16)
    w_g = jnp.pad(w_in[:, n_main:], ((0, 0), (0, LANES - 2 * nh))).astype(BF16)
    b_g = jnp.pad(b_if, (0, LANES - 2 * nh)).reshape(1, LANES)
    outs = [(nh * dk, F32, False), (nh * dk, F32, False), (nh * dv, F32, False), (nh * dv, F32, False),
            (2 * nh, F32, False), (2 * nh, F32, True)]
    q, k, v, og, ig, igt = _row_call(_mlstm_proj_body, [x], [g.reshape(1, d), w_main, w_g, b_g], outs, "mlstm_proj")
    gn2 = gn.reshape(1, nh * dv)
    zeros_c = jnp.zeros((batch, nh, dv, dk), F32)
    zeros_nm = jnp.zeros((batch, nh, 2, dk), F32)
    h_p, c_p, nm_p = _mlstm_scan(q[0], k[0], v[0], og[0], ig[0], igt[0], gn2, zeros_c, zeros_nm, batch, seq,
                                 MLSTM_CHUNK, MLSTM_CHUNKS_PER_STEP, True)
    extra = SUBLANES - dec_seq
    pad_rows = lambda t: jnp.pad(t.reshape(dec_batch, dec_seq, -1), ((0, 0), (0, extra), (0, 0))).reshape(dec_batch * SUBLANES, -1)
    ig_s = ig[1].reshape(dec_batch, dec_seq, 2 * nh)
    pad_gate = jnp.concatenate([jnp.full((dec_batch, extra, nh), MLSTM_NO_INPUT, F32), jnp.zeros((dec_batch, extra, nh), F32)], axis=-1)
    ig_s = jnp.concatenate([ig_s, pad_gate], axis=1).reshape(dec_batch * SUBLANES, 2 * nh)
    nm_s = jnp.stack([n_state, jnp.broadcast_to(m_state[..., None], n_state.shape)], axis=2)
    h_s, c_s, nm_s = _mlstm_scan(pad_rows(q[1]), pad_rows(k[1]), pad_rows(v[1]), pad_rows(og[1]), ig_s, ig_s.T, gn2,
                                 c_state, nm_s, dec_batch, SUBLANES, SUBLANES, MLSTM_SEQS_PER_STEP, False)
    h_s = h_s.reshape(dec_batch, SUBLANES, nh * dv)[:, :dec_seq].reshape(-1, nh * dv)
    (y,) = _row_call(_mlstm_out_body, [x, (h_p, h_s)], [w_out.astype(BF16)], [(d, F32, False)], "mlstm_out")
    state = (c_p, nm_p[:, :, 0], nm_p[:, :, 1, 0], c_s, nm_s[:, :, 0], nm_s[:, :, 1, 0])
    return y, state


def _final_norm_body(rows, consts, outs, is_sample, tile_index):
    outs[0][...] = _rmsnorm(rows[0][...], consts[0][...])


def _final_norm(x, g):
    d = g.shape[0]
    return _row_call(_final_norm_body, [x], [g.reshape(1, d)], [(d, F32, False)], "final_norm")[0]


N_MIXERS = 3


def kernel(x_prompt, x_sample,
           cache_k_l0, cache_v_l0, cache_f_l0,
           state_shift_l1, state_wkv_l1,
           state_c_l2, state_n_l2, state_m_l2,
           cache_k_l3, cache_v_l3, cache_f_l3,
           page_table,
           norm_g, final_g, ffn_w_in, ffn_w_out,
           fox_w_in, fox_b_f, fox_gq, fox_gk, fox_w_out,
           rw_mu, rw_w0, rw_w1, rw_w2, rw_a0, rw_a1, rw_a2, rw_g1, rw_g2,
           rw_kk, rw_ka, rw_rk, rw_w_rkv, rw_w_o, rw_gn_w, rw_gn_b,
           ml_w_in, ml_b_if, ml_gn, ml_w_out):
    batch, seq, d = x_prompt.shape
    dec_batch, dec_seq, _ = x_sample.shape
    dims = (batch, seq, dec_batch, dec_seq)
    depth = norm_g.shape[0]
    x = (x_prompt.reshape(batch * seq, d), x_sample.reshape(dec_batch * dec_seq, d))
    fox_caches = ((cache_k_l0, cache_v_l0, cache_f_l0), (cache_k_l3, cache_v_l3, cache_f_l3))
    rwkv_states = ((state_shift_l1, state_wkv_l1),)
    mlstm_states = ((state_c_l2, state_n_l2, state_m_l2),)
    w_in_bf, w_out_bf = ffn_w_in.astype(BF16), ffn_w_out.astype(BF16)
    layer_state = []
    for i in range(depth):
        kind, j = i % N_MIXERS, i // N_MIXERS
        x = _ffn(x, norm_g[i, 0], w_in_bf[i, 0], w_out_bf[i, 0])
        if kind == 0:
            ck, cv, cf = fox_caches[j]
            x, st = _fox_layer(x, norm_g[i, 1], fox_w_in[j], fox_b_f[j], fox_gq[j], fox_gk[j], fox_w_out[j],
                               ck, cv, cf, page_table, dims)
        elif kind == 1:
            sh_in, wkv_in = rwkv_states[j]
            x, st = _rwkv_layer(x, norm_g[i, 1], sh_in, wkv_in, rw_mu[j], rw_w0[j], rw_w1[j], rw_w2[j], rw_a0[j],
                                rw_a1[j], rw_a2[j], rw_g1[j], rw_g2[j], rw_kk[j], rw_ka[j], rw_rk[j], rw_w_rkv[j],
                                rw_w_o[j], rw_gn_w[j], rw_gn_b[j], dims)
        else:
            c_in, n_in, m_in = mlstm_states[j]
            x, st = _mlstm_layer(x, norm_g[i, 1], c_in, n_in, m_in, ml_w_in[j], ml_b_if[j], ml_gn[j], ml_w_out[j], dims)
        layer_state.append(st)
        x = _ffn(x, norm_g[i, 2], w_in_bf[i, 1], w_out_bf[i, 1])
    y_p, y_s = _final_norm(x, final_g)
    out = (y_p.reshape(batch, seq, d), y_s.reshape(dec_batch, dec_seq, d))
    for st in layer_state:
        out = out + tuple(st)
    return out
```

```python
import functools
import math

import jax
import jax.numpy as jnp
from jax import lax
from jax.experimental import pallas as pl
from jax.experimental.pallas import tpu as pltpu

F32 = jnp.float32
BF16 = jnp.bfloat16

NORM_EPS = 1e-6
VMEM_LIMIT_BYTES = 56 * 1024 * 1024
ROW_TILE = 512
LANES = 128
SUBLANES = 8


def _params(*semantics):
    return pltpu.CompilerParams(dimension_semantics=semantics, vmem_limit_bytes=VMEM_LIMIT_BYTES)


def _const_spec(shape):
    return pl.BlockSpec(shape, lambda *_: (0,) * len(shape), pipeline_mode=pl.Buffered(1))


def _rmsnorm(x, g):
    ms = jnp.mean(x * x, axis=-1, keepdims=True)
    return x * lax.rsqrt(ms + NORM_EPS) * g


def _dot(a, b):
    return jnp.dot(a, b, preferred_element_type=F32)


def _dot_nt(a, b):
    return lax.dot_general(a, b, (((1,), (1,)), ((), ())), preferred_element_type=F32)


def _dot_tn(a, b):
    return lax.dot_general(a, b, (((0,), (0,)), ((), ())), preferred_element_type=F32)


FFN_CHUNK = 256


def _ffn_body(rows, consts, outs, is_sample, tile_index):
    (g_ref, win_ref, wout_ref), (o_ref,) = consts[:3], outs
    d_ff = wout_ref.shape[0]
    x = rows[0][...]
    if len(rows) > 1:
        mixed = rows[1][...]
        if len(rows) > 2:
            mixed = mixed * jax.nn.sigmoid(rows[2][...])
        x = x + _dot(mixed.astype(BF16), consts[3][...])
    xn = _rmsnorm(x, g_ref[...]).astype(BF16)
    acc = jnp.zeros_like(x)
    for c in range(d_ff // FFN_CHUNK):
        lo = c * FFN_CHUNK
        gate = _dot(xn, win_ref[:, lo:lo + FFN_CHUNK])
        up = _dot(xn, win_ref[:, d_ff + lo:d_ff + lo + FFN_CHUNK])
        act = (gate * jax.nn.sigmoid(gate) * up).astype(BF16)
        acc = acc + _dot(act, wout_ref[lo:lo + FFN_CHUNK, :])
    o_ref[...] = x + 0.5 * acc


def _ffn(x, g, w_in, w_out, mixer=None):
    d = w_in.shape[0]
    rows, consts = [x], [g.reshape(1, d), w_in, w_out]
    if mixer is not None:
        rows += mixer[0]
        consts.append(mixer[1].astype(BF16))
    return _row_call(_ffn_body, rows, consts, [(d, F32, False)], "ffn")[0]


def _row_call(body, rows, consts, outs, name, tile_div=1):
    n_p, n_s = rows[0][0].shape[0], rows[0][1].shape[0]
    tile = ROW_TILE // tile_div
    assert n_p % tile == 0 and n_s % tile == 0
    tp, ts = n_p // tile, n_s // tile
    p_idx = lambda i: jnp.minimum(i, tp - 1)
    s_idx = lambda i: jnp.maximum(i - tp, 0)
    per_tile = tile // SUBLANES
    prev_of = lambda t: jnp.maximum(t * per_tile - 1, 0)
    in_specs, args, row_slots = [], [], []
    for entry in rows:
        prev = len(entry) == 3
        slots = []
        for a, idx in ((entry[0], p_idx), (entry[1], s_idx)):
            if a is None:
                slots.append(None)
                continue
            if prev:
                in_specs.append(pl.BlockSpec((SUBLANES, a.shape[1]), lambda i, idx=idx: (prev_of(idx(i)), 0)))
            else:
                in_specs.append(pl.BlockSpec((tile, a.shape[1]), lambda i, idx=idx: (idx(i), 0)))
            slots.append(len(args))
            args.append(a)
        row_slots.append(slots)
    n_row_args = len(args)
    for c in consts:
        in_specs.append(_const_spec(c.shape))
        args.append(c)
    out_specs, out_shape, out_slots = [], [], []
    for cols, dtype, kind in outs:
        slots = []
        if kind == "whole":
            out_specs.append(_const_spec(cols))
            out_shape.append(jax.ShapeDtypeStruct(cols, dtype))
            slots = [len(out_shape) - 1] * 2
        else:
            for n, idx, absent in ((n_p, p_idx, kind == "sample"), (n_s, s_idx, False)):
                if absent:
                    slots.append(None)
                    continue
                if kind is True:
                    out_specs.append(pl.BlockSpec((cols, tile), lambda i, idx=idx: (0, idx(i))))
                    out_shape.append(jax.ShapeDtypeStruct((cols, n), dtype))
                else:
                    out_specs.append(pl.BlockSpec((tile, cols), lambda i, idx=idx: (idx(i), 0)))
                    out_shape.append(jax.ShapeDtypeStruct((n, cols), dtype))
                slots.append(len(out_shape) - 1)
        out_slots.append(slots)
    nc = len(consts)

    def kern(*refs):
        i = pl.program_id(0)
        r, c, o = refs[:n_row_args], refs[n_row_args:n_row_args + nc], refs[n_row_args + nc:]
        pick = lambda seq, slots, side: [None if s[side] is None else seq[s[side]] for s in slots]

        @pl.when(i < tp)
        def _():
            body(pick(r, row_slots, 0), c, pick(o, out_slots, 0), False, i)

        @pl.when(i >= tp)
        def _():
            body(pick(r, row_slots, 1), c, pick(o, out_slots, 1), True, i - tp)

    res = pl.pallas_call(
        kern, grid=(tp + ts,), in_specs=in_specs, out_specs=out_specs, out_shape=out_shape,
        compiler_params=_params("arbitrary"), name=name,
    )(*args)
    result = []
    for (cols, dtype, kind), slots in zip(outs, out_slots):
        if kind == "whole":
            result.append(res[slots[0]])
        else:
            result.append(tuple(None if s is None else res[s] for s in slots))
    return result


def _split3(x):
    hi = x.astype(BF16)
    r1 = x - hi.astype(F32)
    mid = r1.astype(BF16)
    lo = (r1 - mid.astype(F32)).astype(BF16)
    return hi, mid, lo


def _dot01(m01, x):
    hi, mid, lo = _split3(x)
    return _dot(m01, hi) + _dot(m01, mid) + _dot(m01, lo)


def _dot01_r(x, m01):
    hi, mid, lo = _split3(x)
    return _dot(hi, m01) + _dot(mid, m01) + _dot(lo, m01)


def _log_sigmoid(x):
    return jnp.minimum(x, 0.0) - jnp.log1p(jnp.exp(-jnp.abs(x)))


FOX_HEADS = 16
FOX_HEAD_DIM = 64
FOX_WIDTH = FOX_HEADS * FOX_HEAD_DIM
FOX_SCALE = FOX_HEAD_DIM ** -0.5


def _fox_proj_body(rows, consts, outs, is_sample, tile_index):
    (x_ref,) = rows
    g_ref, w_ref, wf_ref, bf_ref = consts
    q_ref, k_ref, v_ref, gate_ref, f_ref, ft_ref = outs
    xn = _rmsnorm(x_ref[...], g_ref[...]).astype(BF16)
    for j, o_ref in enumerate((q_ref, k_ref, v_ref, gate_ref)):
        o_ref[...] = _dot(xn, w_ref[:, j * FOX_WIDTH:(j + 1) * FOX_WIDTH])
    logf = _log_sigmoid(_dot(xn, wf_ref[...]) + bf_ref[...])
    f_ref[...] = logf[:, :FOX_HEADS]
    ft_ref[...] = logf.T[:FOX_HEADS, :]


def _fox_project(x, g, w_in, b_f):
    w_main = w_in[:, :4 * FOX_WIDTH].astype(BF16)
    w_f = jnp.pad(w_in[:, 4 * FOX_WIDTH:], ((0, 0), (0, LANES - FOX_HEADS))).astype(BF16)
    b = jnp.pad(b_f, (0, LANES - FOX_HEADS)).reshape(1, LANES)
    outs = [(FOX_WIDTH, F32, False)] * 4 + [(FOX_HEADS, F32, False), (FOX_HEADS, F32, True)]
    return _row_call(_fox_proj_body, [x], [g.reshape(1, -1), w_main, w_f, b], outs, "fox_proj")


def _fox_cumsum_kernel(f_ref, ft_ref, d_ref, dt_ref, *, seq):
    r = lax.broadcasted_iota(jnp.int32, (LANES, LANES), 0)
    c = lax.broadcasted_iota(jnp.int32, (LANES, LANES), 1)
    lower = jnp.where(c <= r, 1.0, 0.0).astype(BF16)
    upper = jnp.where(r <= c, 1.0, 0.0).astype(BF16)
    carry_c = jnp.zeros((1, FOX_HEADS), F32)
    carry_r = jnp.zeros((FOX_HEADS, 1), F32)
    for blk in range(seq // LANES):
        sl = slice(blk * LANES, (blk + 1) * LANES)
        loc = _dot01(lower, f_ref[sl, :]) + carry_c
        d_ref[sl, :] = loc
        carry_c = loc[LANES - 1:LANES, :]
        loc_r = _dot01_r(ft_ref[:, sl], upper) + carry_r
        dt_ref[:, sl] = loc_r
        carry_r = loc_r[:, LANES - 1:LANES]


def _fox_cumsum(f, ft, batch, seq):
    return pl.pallas_call(
        functools.partial(_fox_cumsum_kernel, seq=seq),
        grid=(batch,),
        in_specs=[pl.BlockSpec((seq, FOX_HEADS), lambda b: (b, 0)),
                  pl.BlockSpec((FOX_HEADS, seq), lambda b: (0, b))],
        out_specs=[pl.BlockSpec((seq, FOX_HEADS), lambda b: (b, 0)),
                   pl.BlockSpec((FOX_HEADS, seq), lambda b: (0, b))],
        out_shape=[jax.ShapeDtypeStruct(f.shape, F32), jax.ShapeDtypeStruct(ft.shape, F32)],
        compiler_params=_params("parallel"),
        name="fox_cumsum",
    )(f, ft)


def _pair_head_norm(x, g, first_head):
    x2 = x * x
    s0 = jnp.sum(jnp.where(first_head, x2, 0.0), axis=-1, keepdims=True)
    s1 = jnp.sum(jnp.where(first_head, 0.0, x2), axis=-1, keepdims=True)
    inv = jnp.where(first_head, lax.rsqrt(s0 / FOX_HEAD_DIM + NORM_EPS), lax.rsqrt(s1 / FOX_HEAD_DIM + NORM_EPS))
    return x * inv * g


FOX_Q_BLOCK = 512


FOX_PAGE_GROUP = 16
FOX_PAGE_SUBGROUP = 4


def _fox_sample_kernel(pt_ref, q_ref, k_ref, v_ref, f_ref, gq_ref, gk_ref, *rest, group, n_groups, t_new):
    kp, vp, fp = rest[:group], rest[group:2 * group], rest[2 * group:3 * group]
    o_ref, kn_ref = rest[3 * group:3 * group + 2]
    q_sc, kn_sc, cq_sc, ck_sc, s_sc, m_sc, l_sc, acc_sc, carry_sc = rest[3 * group + 2:]
    step = pl.program_id(1)
    hd, rows_h = FOX_HEAD_DIM, SUBLANES
    rows, width = q_sc.shape
    page = kp[0].shape[3]
    row_head = lax.broadcasted_iota(jnp.int32, (rows, 1), 0) // rows_h

    @pl.when(step == 0)
    def _():
        own_cols = row_head == lax.broadcasted_iota(jnp.int32, (1, width), 1) // hd
        own_head = row_head == lax.broadcasted_iota(jnp.int32, (1, FOX_HEADS), 1)
        pad_rows = lambda x: jnp.concatenate([x, jnp.zeros((rows_h - t_new, x.shape[1]), F32)], axis=0)
        per_head = lambda x8: jnp.broadcast_to(x8[None], (FOX_HEADS,) + x8.shape).reshape(rows, x8.shape[1])

        def head_norm(x, g):
            xe = jnp.where(own_cols, per_head(pad_rows(x)), 0.0)
            ss = jnp.sum(xe * xe, axis=-1, keepdims=True)
            return xe * lax.rsqrt(ss / hd + NORM_EPS) * g

        q_sc[...] = head_norm(q_ref[0], gq_ref[...]) * FOX_SCALE
        kn = jnp.sum(head_norm(k_ref[0], gk_ref[...]).reshape(FOX_HEADS, rows_h, width), axis=0)
        kn_sc[...] = kn
        kn_ref[0] = kn[:t_new, :]
        f = f_ref[0]
        c_rows = [f[0:1, :]]
        for t in range(1, t_new):
            c_rows.append(c_rows[-1] + f[t:t + 1, :])
        c = jnp.concatenate(c_rows, axis=0)
        cq_sc[...] = jnp.sum(jnp.where(own_head, per_head(pad_rows(c)), 0.0), axis=-1, keepdims=True)
        for j in range(t_new):
            ck_sc[j] = jnp.sum(jnp.where(own_head, c[j:j + 1, :], 0.0), axis=-1, keepdims=True)
        m_sc[...] = jnp.full(m_sc.shape, -jnp.inf, F32)
        l_sc[...] = jnp.zeros(l_sc.shape, F32)
        acc_sc[...] = jnp.zeros(acc_sc.shape, F32)
        carry_sc[...] = jnp.zeros(carry_sc.shape, F32)

    r = lax.broadcasted_iota(jnp.int32, (page, page), 0)
    c_ = lax.broadcasted_iota(jnp.int32, (page, page), 1)
    later = jnp.where(r > c_, 1.0, 0.0).astype(BF16)
    qb = q_sc[...].astype(BF16)
    cq = cq_sc[...]
    carry = carry_sc[...]
    for g in reversed(range(group)):
        ft = fp[g][0]
        suffix = _dot01_r(ft, later) + carry
        carry = carry + jnp.sum(ft, axis=-1, keepdims=True)
        bias = jnp.broadcast_to(suffix[:, None, :], (FOX_HEADS, rows_h, page)).reshape(rows, page)
        kb = kp[g][0].reshape(width, page).astype(BF16)
        s_sc[:, g * page:(g + 1) * page] = _dot(qb, kb) + bias + cq
    carry_sc[...] = carry

    m_run, l_run = m_sc[...], l_sc[...]
    acc = acc_sc[...]
    for first in range(0, group, FOX_PAGE_SUBGROUP):
        pages = range(first, min(first + FOX_PAGE_SUBGROUP, group))
        s_sub = s_sc[:, pages[0] * page:(pages[-1] + 1) * page]
        m_new = jnp.maximum(m_run, jnp.max(s_sub, axis=-1, keepdims=True))
        alpha = jnp.exp(m_run - m_new)
        p_sub = jnp.exp(s_sub - m_new)
        l_run = alpha * l_run + jnp.sum(p_sub, axis=-1, keepdims=True)
        m_run = m_new
        vt = jnp.concatenate([vp[g][0].reshape(width, page).astype(BF16) for g in pages], axis=1)
        acc = alpha * acc + _dot_nt(p_sub.astype(BF16), vt)
    m_sc[...] = m_run
    l_sc[...] = l_run
    acc_sc[...] = acc

    @pl.when(step == n_groups - 1)
    def _():
        t_row = lax.broadcasted_iota(jnp.int32, (rows, 1), 0) % rows_h
        own_cols = row_head == lax.broadcasted_iota(jnp.int32, (1, width), 1) // hd
        q_all = q_sc[...]
        s_new = []
        for j in range(t_new):
            s_j = jnp.sum(q_all * kn_sc[j:j + 1, :], axis=-1, keepdims=True) + cq - ck_sc[j]
            s_new.append(jnp.where(t_row >= j, s_j, -jnp.inf))
        m_new = m_run
        for s_j in s_new:
            m_new = jnp.maximum(m_new, s_j)
        alpha = jnp.exp(m_run - m_new)
        l = alpha * l_run
        o = alpha * acc
        v = v_ref[0]
        for j, s_j in enumerate(s_new):
            p_j = jnp.exp(s_j - m_new)
            l = l + p_j
            o = o + p_j * v[j:j + 1, :]
        o = jnp.where(own_cols, o / l, 0.0)
        o_ref[0] = jnp.sum(o.reshape(FOX_HEADS, rows_h, width), axis=0)[:t_new, :]


def _fox_sample_attention(q, k, v, f, cache_kt, cache_vt, cache_ft, page_table, gq, gk):
    batch, t_new, width = q.shape
    n_pages = page_table.shape[1]
    page = cache_kt.shape[3]
    group = FOX_PAGE_GROUP
    n_groups = n_pages // group
    rows = FOX_HEADS * SUBLANES

    def page_spec(shape, g):
        def index(b, s, pt):
            return (pt[b * n_pages + (n_groups - 1 - s) * group + g],) + (0,) * (len(shape) - 1)
        return pl.BlockSpec((1,) + shape[1:], index)

    tok_spec = lambda last: pl.BlockSpec((1, t_new, last), lambda b, s, pt: (b, 0, 0))
    in_specs = [tok_spec(width), tok_spec(width), tok_spec(width), tok_spec(FOX_HEADS),
                pl.BlockSpec((1, width), lambda b, s, pt: (0, 0)),
                pl.BlockSpec((1, width), lambda b, s, pt: (0, 0))]
    in_specs += [page_spec(cache_kt.shape, g) for g in range(group)]
    in_specs += [page_spec(cache_vt.shape, g) for g in range(group)]
    in_specs += [page_spec(cache_ft.shape, g) for g in range(group)]
    grid_spec = pltpu.PrefetchScalarGridSpec(
        num_scalar_prefetch=1,
        grid=(batch, n_groups),
        in_specs=in_specs,
        out_specs=[tok_spec(width), tok_spec(width)],
        scratch_shapes=[
            pltpu.VMEM((rows, width), F32),
            pltpu.VMEM((SUBLANES, width), F32),
            pltpu.VMEM((rows, 1), F32),
            pltpu.VMEM((t_new, rows, 1), F32),
            pltpu.VMEM((rows, group * page), F32),
            pltpu.VMEM((rows, 1), F32),
            pltpu.VMEM((rows, 1), F32),
            pltpu.VMEM((rows, width), F32),
            pltpu.VMEM((FOX_HEADS, 1), F32),
        ],
    )
    heads_tiled = lambda g: jnp.tile(g, FOX_HEADS).reshape(1, width)
    return pl.pallas_call(
        functools.partial(_fox_sample_kernel, group=group, n_groups=n_groups, t_new=t_new),
        grid_spec=grid_spec,
        out_shape=[jax.ShapeDtypeStruct(q.shape, F32), jax.ShapeDtypeStruct(q.shape, F32)],
        compiler_params=_params("parallel", "arbitrary"),
        name="fox_sample_attention",
    )(page_table.reshape(-1), q, k, v, f, heads_tiled(gq), heads_tiled(gk),
      *([cache_kt] * group), *([cache_vt] * group), *([cache_ft] * group))


LOG2E = 1.4426950408889634
FOX_BIAS_ROWS = 16


def _fox_prompt_kernel(q_ref, k_ref, v_ref, d_ref, dt_ref, gq_ref, gk_ref, o_ref, knt_ref, vt_ref, kt_sc, v_sc, *, tq):
    pair = pl.program_id(1)
    qi = pl.program_id(2)
    first_head = lax.broadcasted_iota(jnp.int32, (1, LANES), 1) < FOX_HEAD_DIM
    n_blk = kt_sc.shape[0]

    @pl.when(qi == 0)
    def _():
        knt = _pair_head_norm(k_ref[...], gk_ref[...], first_head).T
        seq = knt.shape[1]
        knt_ref[0] = knt.reshape(2, FOX_HEAD_DIM, seq)
        sub = lax.broadcasted_iota(jnp.int32, (FOX_BIAS_ROWS, 1), 0)
        bias_rows = jnp.where(sub < 3, 1.0, 0.0).astype(BF16) * jnp.ones((1, seq), BF16)
        for e in (0, 1):
            parts = _split3(dt_ref[pl.ds(2 * pair + e, 1), :] * (-LOG2E))
            for i, part in enumerate(parts):
                bias_rows = jnp.where(sub == 3 + 3 * e + i, part, bias_rows)
        for j in range(n_blk):
            cols = slice(j * tq, (j + 1) * tq)
            kt_sc[j, :LANES, :] = knt[:, cols].astype(BF16)
            kt_sc[j, LANES:, :] = bias_rows[:, cols]
        v = v_ref[...]
        v_sc[...] = v.astype(BF16)
        vt = v.T
        vt_ref[0] = vt.reshape(2, FOX_HEAD_DIM, seq)

    qn = _pair_head_norm(q_ref[...], gq_ref[...], first_head) * (FOX_SCALE * LOG2E)
    d_blk = d_ref[...]
    head_lane = lax.broadcasted_iota(jnp.int32, (1, FOX_HEADS), 1)
    lane_b = lax.broadcasted_iota(jnp.int32, (1, FOX_BIAS_ROWS), 1)
    q_heads = []
    for e in (0, 1):
        d_q = jnp.sum(jnp.where(head_lane == 2 * pair + e, d_blk, 0.0), axis=-1, keepdims=True) * LOG2E
        ones_at = (lane_b >= 3 + 3 * e) & (lane_b < 6 + 3 * e)
        bias_cols = jnp.where(ones_at, 1.0, 0.0).astype(BF16) * jnp.ones((tq, 1), BF16)
        for i, part in enumerate(_split3(d_q)):
            bias_cols = jnp.where(lane_b == i, part, bias_cols)
        mine = first_head if e == 0 else jnp.logical_not(first_head)
        q_heads.append(jnp.concatenate([jnp.where(mine, qn, 0.0).astype(BF16), bias_cols], axis=1))
    row = lax.broadcasted_iota(jnp.int32, (tq, tq), 0)
    col = lax.broadcasted_iota(jnp.int32, (tq, tq), 1)

    def attend(n_kv):
        scores = lambda j: [_dot(q_heads[e], kt_sc[j]) for e in (0, 1)]
        carry = [(jnp.full((tq, 1), -jnp.inf, F32), jnp.zeros((tq, 1), F32), jnp.zeros((tq, LANES), F32))
                 for _ in (0, 1)]
        s_cur = scores(0)
        for j in range(n_kv):
            s_next = scores(j + 1) if j + 1 < n_kv else None
            vb = v_sc[j * tq:(j + 1) * tq, :]
            for e in (0, 1):
                m, l, acc = carry[e]
                s = s_cur[e]
                if j == n_kv - 1:
                    s = jnp.where(col <= row, s, -jnp.inf)
                m_new = jnp.maximum(m, jnp.max(s, axis=-1, keepdims=True))
                alpha = jnp.exp2(m - m_new)
                p = jnp.exp2(s - m_new)
                l = alpha * l + jnp.sum(p, axis=-1, keepdims=True)
                acc = alpha * acc + _dot(p.astype(BF16), vb)
                carry[e] = (m_new, l, acc)
            s_cur = s_next
        (_, l0, acc0), (_, l1, acc1) = carry
        return jnp.where(first_head, acc0 / l0, acc1 / l1)

    for n_kv in range(1, n_blk + 1):
        @pl.when(qi == n_kv - 1)
        def _(n_kv=n_kv):
            o_ref[...] = attend(n_kv)


def _fox_prompt_attention(q, k, v, dcum, dcum_t, gq, gk, batch, seq):
    tq = FOX_Q_BLOCK
    nq = seq // tq
    n = batch * seq
    pairs = FOX_WIDTH // LANES
    gq2 = jnp.tile(gq, 2).reshape(1, LANES)
    gk2 = jnp.tile(gk, 2).reshape(1, LANES)
    t_shape = jax.ShapeDtypeStruct((batch, FOX_HEADS, FOX_HEAD_DIM, seq), F32)
    t_spec = pl.BlockSpec((1, 2, FOX_HEAD_DIM, seq), lambda b, p, i: (b, p, 0, 0))
    return pl.pallas_call(
        functools.partial(_fox_prompt_kernel, tq=tq),
        grid=(batch, pairs, nq),
        in_specs=[
            pl.BlockSpec((tq, LANES), lambda b, p, i: (b * nq + i, p)),
            pl.BlockSpec((seq, LANES), lambda b, p, i: (b, p)),
            pl.BlockSpec((seq, LANES), lambda b, p, i: (b, p)),
            pl.BlockSpec((tq, FOX_HEADS), lambda b, p, i: (b * nq + i, 0)),
            pl.BlockSpec((FOX_HEADS, seq), lambda b, p, i: (0, b)),
            _const_spec((1, LANES)),
            _const_spec((1, LANES)),
        ],
        out_specs=[pl.BlockSpec((tq, LANES), lambda b, p, i: (b * nq + i, p)), t_spec, t_spec],
        out_shape=[jax.ShapeDtypeStruct((n, FOX_WIDTH), F32), t_shape, t_shape],
        scratch_shapes=[pltpu.VMEM((nq, LANES + FOX_BIAS_ROWS, tq), BF16), pltpu.VMEM((seq, LANES), BF16)],
        compiler_params=_params("parallel", "parallel", "arbitrary"),
        name="fox_prompt_attention",
    )(q, k, v, dcum, dcum_t, gq2, gk2)


def _fox_layer(x, g, w_in, b_f, gq, gk, w_out, cache_k, cache_v, cache_f, page_table, dims):
    batch, seq, dec_batch, dec_seq = dims
    q, k, v, gate, f, ft = _fox_project(x, g, w_in, b_f)
    dcum, dcum_t = _fox_cumsum(f[0], ft[0], batch, seq)
    o_p, knt_p, vt_p = _fox_prompt_attention(q[0], k[0], v[0], dcum, dcum_t, gq, gk, batch, seq)
    per_seq = lambda a: a.reshape(dec_batch, dec_seq, a.shape[-1])
    o_s, kn_s = _fox_sample_attention(per_seq(q[1]), per_seq(k[1]), per_seq(v[1]), per_seq(f[1]),
                                      jnp.transpose(cache_k, (0, 2, 3, 1)), jnp.transpose(cache_v, (0, 2, 3, 1)),
                                      jnp.swapaxes(cache_f, 1, 2), page_table, gq, gk)
    o = (o_p, o_s.reshape(-1, FOX_WIDTH))
    heads = lambda a, b, t: a.reshape(b, t, FOX_HEADS, FOX_HEAD_DIM)
    seq_first = lambda a: jnp.transpose(a, (0, 3, 1, 2))
    state = (seq_first(knt_p), seq_first(vt_p), f[0].reshape(batch, seq, FOX_HEADS),
             heads(kn_s, dec_batch, dec_seq), heads(v[1], dec_batch, dec_seq),
             f[1].reshape(dec_batch, dec_seq, FOX_HEADS))
    return ([o, gate], w_out), state


RWKV_HEAD_DIM = 64
RWKV_GN_EPS = 64e-5
RWKV_GATE_LORA_PAD = 256
RWKV_CHUNK = 128
RWKV_CHUNKS_PER_STEP = 8


def _rwkv_proj_body(rows, consts, outs, is_sample, tile_index, *, seq, dec_seq):
    x_ref, prev_ref, start_ref = rows
    (g_ref, mu_ref, w_rkv_ref, w0_ref, w1_ref, w2_ref, a0_ref, a1_ref, a2_ref, g1_ref, g2_ref) = consts
    r_ref, k_ref, v_ref, ld_ref, a_ref, gate_ref, xn_ref, last_ref = outs
    tile = x_ref.shape[0]
    xn = _rmsnorm(x_ref[...], g_ref[...])
    rolled = pltpu.roll(xn, 1, 0)
    row = lax.broadcasted_iota(jnp.int32, (tile, 1), 0)
    if is_sample:
        xn_ref[...] = xn
        x_prev = jnp.where(row % dec_seq == 0, start_ref[...], rolled)
    else:
        before = _rmsnorm(prev_ref[SUBLANES - 1:SUBLANES, :], g_ref[...])
        before = jnp.where((tile_index * tile) % seq == 0, 0.0, before)
        x_prev = jnp.where(row == 0, before, rolled)

        @pl.when(((tile_index + 1) * tile) % seq == 0)
        def _():
            last_ref[pl.ds((tile_index * tile) // seq, 1), :] = xn[tile - 1:tile, :]

    xx = x_prev - xn
    mix = lambda i: (xn + xx * mu_ref[i:i + 1, :]).astype(BF16)
    r_ref[...] = _dot(mix(0), w_rkv_ref[0])
    k_ref[...] = _dot(mix(2), w_rkv_ref[1])
    v_ref[...] = _dot(mix(3), w_rkv_ref[2])
    wl = w0_ref[...] + _dot(jnp.tanh(_dot(mix(1), w1_ref[...])).astype(BF16), w2_ref[...])
    softplus_neg = jnp.maximum(-wl, 0.0) + jnp.log1p(jnp.exp(-jnp.abs(wl)))
    ld_ref[...] = -jnp.exp(-softplus_neg - 0.5)
    a_ref[...] = jax.nn.sigmoid(a0_ref[...] + _dot(_dot(mix(4), a1_ref[...]).astype(BF16), a2_ref[...]))
    gate_ref[...] = _dot(jax.nn.sigmoid(_dot(mix(5), g1_ref[...])).astype(BF16), g2_ref[...])


def _dot3(a, b):
    a_hi = a.astype(BF16)
    a_lo = (a - a_hi.astype(F32)).astype(BF16)
    b_hi = b.astype(BF16)
    b_lo = (b - b_hi.astype(F32)).astype(BF16)
    return _dot(a_hi, b_hi) + _dot(a_hi, b_lo) + _dot(a_lo, b_hi)


def _pair_sums(x, first_head):
    s0 = jnp.sum(jnp.where(first_head, x, 0.0), axis=-1, keepdims=True)
    s1 = jnp.sum(jnp.where(first_head, 0.0, x), axis=-1, keepdims=True)
    return jnp.where(first_head, s0, s1)


def _rwkv_chunk_terms(chunks, kk_w, ka_w):
    length = chunks[0][0].shape[0]
    first_head = lax.broadcasted_iota(jnp.int32, (1, LANES), 1) < RWKV_HEAD_DIM
    head_masks = (first_head, jnp.logical_not(first_head))
    ti = lax.broadcasted_iota(jnp.int32, (length, length), 0)
    si = lax.broadcasted_iota(jnp.int32, (length, length), 1)
    strict, incl = si < ti, si <= ti
    lower01 = jnp.where(incl, 1.0, 0.0).astype(BF16)
    levels = max(1, math.ceil(math.log2(length)))
    insts = [(c, e) for c in range(len(chunks)) for e in (0, 1)]

    cl = [_dot01(lower01, ld) for (_, _, _, ld, _) in chunks]
    pre = []
    for (r, k, v, ld, a), cl_c in zip(chunks, cl):
        kk = k * kk_w
        kkn = kk / jnp.maximum(jnp.sqrt(_pair_sums(kk * kk, first_head)), 1e-12)
        k2 = k * (1.0 + (a - 1.0) * ka_w)
        aa, bb = -kkn, kkn * a
        cl_last = cl_c[length - 1:length, :]
        cl_mid = cl_c[length // 2 - 1:length // 2, :]
        p_out = jnp.exp(cl_mid - cl_c)
        p_in = jnp.exp(-cl_mid)
        p_end = jnp.exp(cl_last - cl_c)
        r_t, a_t = r * jnp.exp(cl_c), aa * jnp.exp(cl_c - ld)
        pre.append(dict(
            k2=k2, vb=v.astype(BF16), r_t=r_t, a_t=a_t,
            lhs=jnp.concatenate([a_t * p_in, r_t * p_in], axis=0),
            rhs=jnp.concatenate([bb * p_out, k2 * p_out], axis=0).astype(BF16),
            b_end=(bb * p_end).astype(BF16), k_end=(k2 * p_end).astype(BF16), decay_end=jnp.exp(cl_last)))
    a_e = [jnp.where(head_masks[e], pre[c]["a_t"], 0.0) for c, e in insts]
    r_e = [jnp.where(head_masks[e], pre[c]["r_t"], 0.0) for c, e in insts]
    mm = [_dot_nt(jnp.where(head_masks[e], pre[c]["lhs"], 0.0).astype(BF16), pre[c]["rhs"])
          for c, e in insts]
    m_ak = [jnp.where(strict, m[:length, length:], 0.0).astype(BF16) for m in mm]
    m_rb = [jnp.where(incl, m[length:, :length], 0.0).astype(BF16) for m in mm]
    m_rk = [jnp.where(incl, m[length:, length:], 0.0).astype(BF16) for m in mm]
    power = [jnp.where(strict, m[:length, :length], 0.0) for m in mm]
    x = [jnp.concatenate([a_e[i], _dot(m_ak[i], pre[c]["vb"])], axis=1) for i, (c, e) in enumerate(insts)]
    for lvl in range(levels):
        pb = [p_i.astype(BF16) for p_i in power]
        x = [x_i + _dot(p_i, x_i.astype(BF16)) for x_i, p_i in zip(x, pb)]
        if lvl + 1 < levels:
            power = [_dot(p_i, p_i) for p_i in pb]
    w_e = [x_i[:, :LANES] for x_i in x]
    u0_e = [x_i[:, LANES:] for x_i in x]
    yr_e = [r_e[i] + _dot(m_rb[i], w_e[i].astype(BF16)) for i in range(len(insts))]
    y0_e = [_dot(m_rb[i], u0_e[i].astype(BF16)) + _dot(m_rk[i], pre[c]["vb"]) for i, (c, e) in enumerate(insts)]
    row = lax.broadcasted_iota(jnp.int32, (LANES, LANES), 0)
    col = lax.broadcasted_iota(jnp.int32, (LANES, LANES), 1)
    same_head = (row < RWKV_HEAD_DIM) == (col < RWKV_HEAD_DIM)
    out = []
    for c in range(len(chunks)):
        i0, i1 = 2 * c, 2 * c + 1
        w = (w_e[i0] + w_e[i1]).astype(BF16)
        u0 = jnp.where(first_head, u0_e[i0], u0_e[i1]).astype(BF16)
        g = jnp.where(same_head, jnp.where(row == col, pre[c]["decay_end"], 0.0) + _dot_tn(pre[c]["b_end"], w), 0.0)
        h = jnp.where(same_head, _dot_tn(pre[c]["b_end"], u0) + _dot_tn(pre[c]["k_end"], pre[c]["vb"]), 0.0)
        out.append((g, h, yr_e[i0] + yr_e[i1], jnp.where(first_head, y0_e[i0], y0_e[i1]), pre[c]["k2"]))
    return out


def _rwkv_finish(y, r, k2, v, gate, rk_w, gnw, gnb):
    first_head = lax.broadcasted_iota(jnp.int32, (1, LANES), 1) < RWKV_HEAD_DIM
    mean = _pair_sums(y, first_head) / RWKV_HEAD_DIM
    yc = y - mean
    var = _pair_sums(yc * yc, first_head) / RWKV_HEAD_DIM
    yn = yc * lax.rsqrt(var + RWKV_GN_EPS) * gnw + gnb
    bonus = _pair_sums(r * k2 * rk_w, first_head) * v
    return (yn + bonus) * gate


def _rwkv_scan_kernel(r_ref, k_ref, v_ref, ld_ref, a_ref, gate_ref, kk_ref, ka_ref, rk_ref, gnw_ref, gnb_ref,
                      z_ref, s_ref, state_sc, *, length, n_inst):
    step = pl.program_id(2)

    @pl.when(step == 0)
    def _():
        state_sc[...] = jnp.zeros(state_sc.shape, F32)

    state = state_sc[...]
    rows = [slice(i * length, (i + 1) * length) for i in range(n_inst)]
    chunks = [(r_ref[rs, :], k_ref[rs, :], v_ref[rs, :], ld_ref[rs, :], a_ref[rs, :]) for rs in rows]
    terms = _rwkv_chunk_terms(chunks, kk_ref[...], ka_ref[...])
    for rs, (r, k, v, ld, a), (g, h, yr, y0, k2) in zip(rows, chunks, terms):
        y = _dot(yr.astype(BF16), state.astype(BF16)) + y0
        state = _dot3(g, state) + h
        z_ref[rs, :] = _rwkv_finish(y, r, k2, v, gate_ref[rs, :], rk_ref[...], gnw_ref[...], gnb_ref[...])
    state_sc[...] = state

    @pl.when(step == pl.num_programs(2) - 1)
    def _():
        st = state.T
        s_ref[0, 0] = st[:RWKV_HEAD_DIM, :RWKV_HEAD_DIM]
        s_ref[0, 1] = st[RWKV_HEAD_DIM:, RWKV_HEAD_DIM:]


def _rwkv_scan(r, k, v, ld, a, gate, params, n_seq, seq):
    d = r.shape[1]
    pairs = d // LANES
    heads = d // RWKV_HEAD_DIM
    rows = RWKV_CHUNKS_PER_STEP * RWKV_CHUNK
    steps = seq // rows
    row_spec = pl.BlockSpec((rows, LANES), lambda b, p, s: (b * steps + s, p))
    par_spec = pl.BlockSpec((1, LANES), lambda b, p, s: (0, p))
    st_spec = pl.BlockSpec((1, 2, RWKV_HEAD_DIM, RWKV_HEAD_DIM), lambda b, p, s: (b, p, 0, 0))
    return pl.pallas_call(
        functools.partial(_rwkv_scan_kernel, length=RWKV_CHUNK, n_inst=RWKV_CHUNKS_PER_STEP),
        grid=(n_seq, pairs, steps),
        in_specs=[row_spec] * 6 + [par_spec] * 5,
        out_specs=[row_spec, st_spec],
        out_shape=[jax.ShapeDtypeStruct(r.shape, F32),
                   jax.ShapeDtypeStruct((n_seq, heads, RWKV_HEAD_DIM, RWKV_HEAD_DIM), F32)],
        scratch_shapes=[pltpu.VMEM((LANES, LANES), F32)],
        compiler_params=_params("parallel", "parallel", "arbitrary"),
        name="rwkv_scan_prompt",
    )(r, k, v, ld, a, gate, *params)


def _rwkv_sample_kernel(r_ref, k_ref, v_ref, ld_ref, a_ref, gate_ref, kk_ref, ka_ref, rk_ref, gnw_ref, gnb_ref,
                        s0_ref, z_ref, s_ref, y_sc, *, steps):
    n = RWKV_HEAD_DIM
    s_ref[...] = s0_ref[...]
    kk_w, ka_w, rk_w, gnw, gnb = kk_ref[0], ka_ref[0], rk_ref[0], gnw_ref[0], gnb_ref[0]
    for t in range(steps):
        r, k, v, a = r_ref[t, 0], k_ref[t, 0], v_ref[t, 0], a_ref[t, 0]
        decay = jnp.exp(ld_ref[t, 0])
        kk = k * kk_w
        kkn = kk / jnp.maximum(jnp.sqrt(jnp.sum(kk * kk, axis=0, keepdims=True)), 1e-12)
        k2 = k * (1.0 + (a - 1.0) * ka_w)
        aa, bb = -kkn, kkn * a

        def value_row(i, carry):
            s_i = s_ref[0, i]
            sa = jnp.sum(s_i * aa, axis=0, keepdims=True)
            v_i = v_ref[t, 0, pl.ds(i, 1), :]
            s_i = s_i * decay + sa * bb + v_i * k2
            s_ref[0, i] = s_i
            y_sc[pl.ds(i, 1), :] = jnp.sum(s_i * r, axis=0, keepdims=True)
            return carry

        lax.fori_loop(0, n, value_row, 0, unroll=4)
        y = y_sc[...]
        mean = jnp.mean(y, axis=0, keepdims=True)
        yc = y - mean
        var = jnp.mean(yc * yc, axis=0, keepdims=True)
        yn = yc * lax.rsqrt(var + RWKV_GN_EPS) * gnw + gnb
        bonus = jnp.sum(r * k2 * rk_w, axis=0, keepdims=True) * v
        z_ref[t, 0] = (yn + bonus) * gate_ref[t, 0]


def _rwkv_sample(r, k, v, ld, a, gate, s0, params, dec_batch, dec_seq):
    d = r.shape[1]
    n = RWKV_HEAD_DIM
    heads = d // n
    lanes_last = lambda t: jnp.transpose(t.reshape(dec_batch, dec_seq, heads, n), (1, 2, 3, 0))
    tok_spec = pl.BlockSpec((dec_seq, 1, n, dec_batch), lambda h: (0, h, 0, 0))
    par_spec = pl.BlockSpec((1, n, 1), lambda h: (h, 0, 0))
    st_spec = pl.BlockSpec((1, n, n, dec_batch), lambda h: (h, 0, 0, 0))
    z, s = pl.pallas_call(
        functools.partial(_rwkv_sample_kernel, steps=dec_seq),
        grid=(heads,),
        in_specs=[tok_spec] * 6 + [par_spec] * 5 + [st_spec],
        out_specs=[tok_spec, st_spec],
        out_shape=[jax.ShapeDtypeStruct((dec_seq, heads, n, dec_batch), F32),
                   jax.ShapeDtypeStruct((heads, n, n, dec_batch), F32)],
        scratch_shapes=[pltpu.VMEM((n, dec_batch), F32)],
        compiler_params=_params("parallel"),
        name="rwkv_scan_sample",
    )(*[lanes_last(t) for t in (r, k, v, ld, a, gate)], *[p.reshape(heads, n, 1) for p in params],
      jnp.transpose(s0, (1, 2, 3, 0)))
    return jnp.transpose(z, (3, 0, 1, 2)).reshape(dec_batch * dec_seq, d), jnp.transpose(s, (3, 0, 1, 2))


def _rwkv_layer(x, g, shift_state, wkv_state, mu, w0, w1, w2, a0, a1, a2, g1, g2, kk_w, ka_w, rk_w, w_rkv, w_o,
                gn_w, gn_b, dims):
    batch, seq, dec_batch, dec_seq = dims
    d = mu.shape[1]
    heads = d // RWKV_HEAD_DIM
    row = lambda t: t.reshape(1, d)
    starts = jnp.concatenate([shift_state[:, None, :], jnp.zeros((dec_batch, dec_seq - 1, d), F32)], axis=1)
    rank = g1.shape[1]
    g1p = jnp.pad(g1, ((0, 0), (0, RWKV_GATE_LORA_PAD - rank))).astype(BF16)
    g2p = jnp.pad(g2, ((0, RWKV_GATE_LORA_PAD - rank), (0, 0))).astype(BF16)
    consts = [row(g), mu, w_rkv.astype(BF16), row(w0), w1.astype(BF16), w2.astype(BF16), row(a0), a1.astype(BF16),
              a2.astype(BF16), g1p, g2p]
    outs = [(d, F32, False)] * 6 + [(d, F32, "sample"), ((batch, d), F32, "whole")]
    r, k, v, ld, a, gate, xn, last_p = _row_call(
        functools.partial(_rwkv_proj_body, seq=seq, dec_seq=dec_seq),
        [x, (x[0], None, "prev"), (None, starts.reshape(-1, d))], consts, outs, "rwkv_proj", tile_div=2)
    params = [row(t) for t in (kk_w, ka_w, rk_w.reshape(-1), gn_w, gn_b)]
    z_p, s_p = _rwkv_scan(r[0], k[0], v[0], ld[0], a[0], gate[0], params, batch, seq)
    z_s, s_s = _rwkv_sample(r[1], k[1], v[1], ld[1], a[1], gate[1], wkv_state, params, dec_batch, dec_seq)
    shift_s = xn[1].reshape(dec_batch, dec_seq, d)[:, dec_seq - 1]
    return ([(z_p, z_s)], w_o), (last_p, s_p, shift_s, s_s)


MLSTM_HEADS = 4
MLSTM_QK_DIM = 128
MLSTM_V_DIM = 256
MLSTM_CHUNK = 64
MLSTM_GATE_CAP = 15.0
MLSTM_CHUNKS_PER_STEP = 8
MLSTM_SEQS_PER_STEP = 16
MLSTM_NO_INPUT = -1e30


def _mlstm_proj_body(rows, consts, outs, is_sample, tile_index):
    (x_ref,), (g_ref, w_ref, wg_ref, bg_ref) = rows, consts
    q_ref, k_ref, v_ref, og_ref, ig_ref, igt_ref = outs
    hk = MLSTM_HEADS * MLSTM_QK_DIM
    hv = MLSTM_HEADS * MLSTM_V_DIM
    xn = _rmsnorm(x_ref[...], g_ref[...]).astype(BF16)
    q_ref[...] = _dot(xn, w_ref[:, :hk]) * MLSTM_QK_DIM ** -0.5
    k_ref[...] = _dot(xn, w_ref[:, hk:2 * hk])
    v_ref[...] = _dot(xn, w_ref[:, 2 * hk:2 * hk + hv])
    og_ref[...] = _dot(xn, w_ref[:, 2 * hk + hv:])
    gates = MLSTM_GATE_CAP * jnp.tanh((_dot(xn, wg_ref[...]) + bg_ref[...]) / MLSTM_GATE_CAP)
    is_forget = lax.broadcasted_iota(jnp.int32, (1, LANES), 1) >= MLSTM_HEADS
    ig = jnp.where(is_forget, _log_sigmoid(gates), gates)
    ig_ref[...] = ig[:, :2 * MLSTM_HEADS]
    igt_ref[...] = ig.T[:2 * MLSTM_HEADS, :]


def _mlstm_chunks(chunks, states, head, gn, chained):
    length = chunks[0][0].shape[0]
    nh = MLSTM_HEADS
    count = len(chunks)
    ti = lax.broadcasted_iota(jnp.int32, (length, length), 0)
    si = lax.broadcasted_iota(jnp.int32, (length, length), 1)
    causal = si <= ti
    lower01 = jnp.where(causal, 1.0, 0.0).astype(BF16)
    upper01 = jnp.where(ti <= si, 1.0, 0.0).astype(BF16)
    lane_g = lax.broadcasted_iota(jnp.int32, (1, 2 * nh), 1)
    sub_g = lax.broadcasted_iota(jnp.int32, (2 * nh, 1), 0)
    col = lambda x, j: jnp.sum(jnp.where(lane_g == j, x, 0.0), axis=-1, keepdims=True)
    rowv = lambda x, j: jnp.sum(jnp.where(sub_g == j, x, 0.0), axis=0, keepdims=True)
    last = slice(length - 1, length)

    cum_c = [_dot01(lower01, ch[4]) for ch in chunks]
    cum_r = [_dot01_r(ch[5], upper01) for ch in chunks]
    b_c = [col(x, nh + head) for x in cum_c]
    li_c = [col(ch[4], head) for ch in chunks]
    logw = [jnp.where(causal, b - rowv(xr, nh + head) + rowv(ch[5], head), -jnp.inf)
            for b, xr, ch in zip(b_c, cum_r, chunks)]
    max_logw = [jnp.max(x, axis=-1, keepdims=True) for x in logw]
    qb = [ch[0].astype(BF16) for ch in chunks]
    kb = [ch[1].astype(BF16) for ch in chunks]
    vb = [ch[2].astype(BF16) for ch in chunks]
    qk = [_dot_nt(a, b) for a, b in zip(qb, kb)]
    m_in, m_t = [], []
    for i in range(count):
        m_prev = states[i][2] if (not chained or i == 0) else m_t[i - 1][last, :]
        m_in.append(m_prev)
        m_t.append(jnp.maximum(m_prev + b_c[i], max_logw[i]))
    inter = [m + b for m, b in zip(m_in, b_c)]
    inter_scale = [jnp.exp(a - b) for a, b in zip(inter, m_t)]
    sw = [jnp.exp(lw - mt) * s for lw, mt, s in zip(logw, m_t, qk)]
    swv = [_dot(s.astype(BF16), v) for s, v in zip(sw, vb)]
    m_new = [mt[last, :] for mt in m_t]
    carry_scale = [jnp.exp(a[last, :] - b) for a, b in zip(inter, m_new)]
    w_s = [jnp.exp(b[last, :] - b + li - mn) for b, li, mn in zip(b_c, li_c, m_new)]
    kv = [_dot_tn((ch[2] * w).astype(BF16), k) for ch, w, k in zip(chunks, w_s, kb)]
    ksum = [jnp.sum(w * ch[1], axis=0, keepdims=True) for ch, w in zip(chunks, w_s)]
    c_in, n_in, after = [], [], []
    for i in range(count):
        c_prev, n_prev = (states[i][0], states[i][1]) if (not chained or i == 0) else after[i - 1][:2]
        c_in.append(c_prev)
        n_in.append(n_prev)
        after.append((carry_scale[i] * c_prev + kv[i], carry_scale[i] * n_prev + ksum[i], m_new[i]))
    outs = []
    for i, ch in enumerate(chunks):
        num = inter_scale[i] * _dot_nt(qb[i], c_in[i].astype(BF16)) + swv[i]
        den = inter_scale[i] * jnp.sum(ch[0] * n_in[i], axis=-1, keepdims=True) + jnp.sum(sw[i], axis=-1, keepdims=True)
        h = num / jnp.maximum(jnp.abs(den), jnp.exp(-m_t[i]))
        h = h * lax.rsqrt(jnp.mean(h * h, axis=-1, keepdims=True) + NORM_EPS)
        outs.append(h * gn * jax.nn.sigmoid(ch[3]))
    return outs, after


def _mlstm_scan_kernel(q_ref, k_ref, v_ref, og_ref, ig_ref, igt_ref, gn_ref, c0_ref, nm0_ref,
                       h_ref, c_ref, nm_ref, c_sc, nm_sc, *, length, n_inst, chained):
    head = pl.program_id(1)
    step = pl.program_id(2)
    rows = [slice(i * length, (i + 1) * length) for i in range(n_inst)]
    chunks = [(q_ref[rs, :], k_ref[rs, :], v_ref[rs, :], og_ref[rs, :], ig_ref[rs, :], igt_ref[:, rs]) for rs in rows]
    pack_nm = lambda n, m: jnp.concatenate([n, jnp.broadcast_to(m, n.shape)], axis=0)
    if chained:
        @pl.when(step == 0)
        def _():
            c_sc[...] = c0_ref[0, 0]
            nm_sc[...] = nm0_ref[0, 0]
        states = [(c_sc[...], nm_sc[0:1, :], nm_sc[1:2, 0:1])]
    else:
        states = [(c0_ref[i, 0], nm0_ref[i, 0, 0:1, :], nm0_ref[i, 0, 1:2, 0:1]) for i in range(n_inst)]
    outs, after = _mlstm_chunks(chunks, states, head, gn_ref[...], chained)
    for rs, h in zip(rows, outs):
        h_ref[rs, :] = h
    if chained:
        c, n, m = after[-1]
        c_sc[...] = c
        nm_sc[...] = pack_nm(n, m)

        @pl.when(step == pl.num_programs(2) - 1)
        def _():
            c_ref[0, 0] = c
            nm_ref[0, 0] = pack_nm(n, m)
    else:
        for i, (c, n, m) in enumerate(after):
            c_ref[i, 0] = c
            nm_ref[i, 0] = pack_nm(n, m)


def _mlstm_scan(q, k, v, og, ig, igt, gn, c0, nm0, n_seq, seq, length, n_inst, chained):
    dk, dv, nh = MLSTM_QK_DIM, MLSTM_V_DIM, MLSTM_HEADS
    rows = n_inst * length
    if chained:
        steps = seq // rows
        grid = (n_seq, nh, steps)
        r_idx = lambda b, h, s: b * steps + s
        sb = 1
    else:
        assert seq == length
        grid = (n_seq // n_inst, nh, 1)
        r_idx = lambda b, h, s: b
        sb = n_inst
    c_spec = pl.BlockSpec((sb, 1, dv, dk), lambda b, h, s: (b, h, 0, 0))
    nm_spec = pl.BlockSpec((sb, 1, 2, dk), lambda b, h, s: (b, h, 0, 0))
    return pl.pallas_call(
        functools.partial(_mlstm_scan_kernel, length=length, n_inst=n_inst, chained=chained),
        grid=grid,
        in_specs=[
            pl.BlockSpec((rows, dk), lambda b, h, s: (r_idx(b, h, s), h)),
            pl.BlockSpec((rows, dk), lambda b, h, s: (r_idx(b, h, s), h)),
            pl.BlockSpec((rows, dv), lambda b, h, s: (r_idx(b, h, s), h)),
            pl.BlockSpec((rows, dv), lambda b, h, s: (r_idx(b, h, s), h)),
            pl.BlockSpec((rows, 2 * nh), lambda b, h, s: (r_idx(b, h, s), 0)),
            pl.BlockSpec((2 * nh, rows), lambda b, h, s: (0, r_idx(b, h, s))),
            pl.BlockSpec((1, dv), lambda b, h, s: (0, h)),
            c_spec, nm_spec,
        ],
        out_specs=[pl.BlockSpec((rows, dv), lambda b, h, s: (r_idx(b, h, s), h)), c_spec, nm_spec],
        out_shape=[jax.ShapeDtypeStruct(v.shape, F32), jax.ShapeDtypeStruct((n_seq, nh, dv, dk), F32),
                   jax.ShapeDtypeStruct((n_seq, nh, 2, dk), F32)],
        scratch_shapes=[pltpu.VMEM((dv, dk), F32), pltpu.VMEM((2, dk), F32)],
        compiler_params=_params("parallel", "parallel", "arbitrary"),
        name="mlstm_scan_prompt" if chained else "mlstm_scan_sample",
    )(q, k, v, og, ig, igt, gn, c0, nm0)


def _mlstm_layer(x, g, c_state, n_state, m_state, w_in, b_if, gn, w_out, dims):
    batch, seq, dec_batch, dec_seq = dims
    dk, dv, nh = MLSTM_QK_DIM, MLSTM_V_DIM, MLSTM_HEADS
    d = w_in.shape[0]
    n_main = 2 * nh * dk + 2 * nh * dv
    w_main = w_in[:, :n_main].astype(BF16)
    w_g = jnp.pad(w_in[:, n_main:], ((0, 0), (0, LANES - 2 * nh))).astype(BF16)
    b_g = jnp.pad(b_if, (0, LANES - 2 * nh)).reshape(1, LANES)
    outs = [(nh * dk, F32, False), (nh * dk, F32, False), (nh * dv, F32, False), (nh * dv, F32, False),
            (2 * nh, F32, False), (2 * nh, F32, True)]
    q, k, v, og, ig, igt = _row_call(_mlstm_proj_body, [x], [g.reshape(1, d), w_main, w_g, b_g], outs, "mlstm_proj")
    gn2 = gn.reshape(1, nh * dv)
    zeros_c = jnp.zeros((batch, nh, dv, dk), F32)
    zeros_nm = jnp.zeros((batch, nh, 2, dk), F32)
    h_p, c_p, nm_p = _mlstm_scan(q[0], k[0], v[0], og[0], ig[0], igt[0], gn2, zeros_c, zeros_nm, batch, seq,
                                 MLSTM_CHUNK, MLSTM_CHUNKS_PER_STEP, True)
    extra = SUBLANES - dec_seq
    pad_rows = lambda t: jnp.pad(t.reshape(dec_batch, dec_seq, -1), ((0, 0), (0, extra), (0, 0))).reshape(dec_batch * SUBLANES, -1)
    ig_s = ig[1].reshape(dec_batch, dec_seq, 2 * nh)
    pad_gate = jnp.concatenate([jnp.full((dec_batch, extra, nh), MLSTM_NO_INPUT, F32), jnp.zeros((dec_batch, extra, nh), F32)], axis=-1)
    ig_s = jnp.concatenate([ig_s, pad_gate], axis=1).reshape(dec_batch * SUBLANES, 2 * nh)
    nm_s = jnp.stack([n_state, jnp.broadcast_to(m_state[..., None], n_state.shape)], axis=2)
    h_s, c_s, nm_s = _mlstm_scan(pad_rows(q[1]), pad_rows(k[1]), pad_rows(v[1]), pad_rows(og[1]), ig_s, ig_s.T, gn2,
                                 c_state, nm_s, dec_batch, SUBLANES, SUBLANES, MLSTM_SEQS_PER_STEP, False)
    h_s = h_s.reshape(dec_batch, SUBLANES, nh * dv)[:, :dec_seq].reshape(-1, nh * dv)
    state = (c_p, nm_p[:, :, 0], nm_p[:, :, 1, 0], c_s, nm_s[:, :, 0], nm_s[:, :, 1, 0])
    return ([(h_p, h_s)], w_out), state


def _final_norm_body(rows, consts, outs, is_sample, tile_index):
    outs[0][...] = _rmsnorm(rows[0][...], consts[0][...])


def _final_norm(x, g):
    d = g.shape[0]
    return _row_call(_final_norm_body, [x], [g.reshape(1, d)], [(d, F32, False)], "final_norm")[0]


N_MIXERS = 3


def kernel(x_prompt, x_sample,
           cache_k_l0, cache_v_l0, cache_f_l0,
           state_shift_l1, state_wkv_l1,
           state_c_l2, state_n_l2, state_m_l2,
           cache_k_l3, cache_v_l3, cache_f_l3,
           page_table,
           norm_g, final_g, ffn_w_in, ffn_w_out,
           fox_w_in, fox_b_f, fox_gq, fox_gk, fox_w_out,
           rw_mu, rw_w0, rw_w1, rw_w2, rw_a0, rw_a1, rw_a2, rw_g1, rw_g2,
           rw_kk, rw_ka, rw_rk, rw_w_rkv, rw_w_o, rw_gn_w, rw_gn_b,
           ml_w_in, ml_b_if, ml_gn, ml_w_out):
    batch, seq, d = x_prompt.shape
    dec_batch, dec_seq, _ = x_sample.shape
    dims = (batch, seq, dec_batch, dec_seq)
    depth = norm_g.shape[0]
    x = (x_prompt.reshape(batch * seq, d), x_sample.reshape(dec_batch * dec_seq, d))
    fox_caches = ((cache_k_l0, cache_v_l0, cache_f_l0), (cache_k_l3, cache_v_l3, cache_f_l3))
    rwkv_states = ((state_shift_l1, state_wkv_l1),)
    mlstm_states = ((state_c_l2, state_n_l2, state_m_l2),)
    w_in_bf, w_out_bf = ffn_w_in.astype(BF16), ffn_w_out.astype(BF16)
    layer_state = []
    for i in range(depth):
        kind, j = i % N_MIXERS, i // N_MIXERS
        x = _ffn(x, norm_g[i, 0], w_in_bf[i, 0], w_out_bf[i, 0])
        if kind == 0:
            ck, cv, cf = fox_caches[j]
            mixer, st = _fox_layer(x, norm_g[i, 1], fox_w_in[j], fox_b_f[j], fox_gq[j], fox_gk[j], fox_w_out[j],
                                   ck, cv, cf, page_table, dims)
        elif kind == 1:
            sh_in, wkv_in = rwkv_states[j]
            mixer, st = _rwkv_layer(x, norm_g[i, 1], sh_in, wkv_in, rw_mu[j], rw_w0[j], rw_w1[j], rw_w2[j], rw_a0[j],
                                    rw_a1[j], rw_a2[j], rw_g1[j], rw_g2[j], rw_kk[j], rw_ka[j], rw_rk[j], rw_w_rkv[j],
                                    rw_w_o[j], rw_gn_w[j], rw_gn_b[j], dims)
        else:
            c_in, n_in, m_in = mlstm_states[j]
            mixer, st = _mlstm_layer(x, norm_g[i, 1], c_in, n_in, m_in, ml_w_in[j], ml_b_if[j], ml_gn[j],
                                     ml_w_out[j], dims)
        layer_state.append(st)
        x = _ffn(x, norm_g[i, 2], w_in_bf[i, 1], w_out_bf[i, 1], mixer)
    y_p, y_s = _final_norm(x, final_g)
    out = (y_p.reshape(batch, seq, d), y_s.reshape(dec_batch, dec_seq, d))
    for st in layer_state:
        out = out + tuple(st)
    return out
```

```python
import functools
import math

import jax
import jax.numpy as jnp
from jax import lax
from jax.experimental import pallas as pl
from jax.experimental.pallas import tpu as pltpu

F32 = jnp.float32
BF16 = jnp.bfloat16

NORM_EPS = 1e-6
VMEM_LIMIT_BYTES = 56 * 1024 * 1024
ROW_TILE = 512
LANES = 128
SUBLANES = 8


def _params(*semantics):
    return pltpu.CompilerParams(dimension_semantics=semantics, vmem_limit_bytes=VMEM_LIMIT_BYTES)


def _const_spec(shape):
    return pl.BlockSpec(shape, lambda *_: (0,) * len(shape), pipeline_mode=pl.Buffered(1))


def _rmsnorm(x, g):
    ms = jnp.mean(x * x, axis=-1, keepdims=True)
    return x * lax.rsqrt(ms + NORM_EPS) * g


def _dot(a, b):
    return jnp.dot(a, b, preferred_element_type=F32)


def _dot_nt(a, b):
    return lax.dot_general(a, b, (((1,), (1,)), ((), ())), preferred_element_type=F32)


def _dot_tn(a, b):
    return lax.dot_general(a, b, (((0,), (0,)), ((), ())), preferred_element_type=F32)


FFN_CHUNK = 256


def _ffn_body(rows, consts, outs, is_sample, tile_index, *, final):
    (g_ref, win_ref, wout_ref), (o_ref,) = consts[:3], outs
    d_ff = wout_ref.shape[0]
    x = rows[0][...]
    if len(rows) > 1:
        mixed = rows[1][...]
        if len(rows) > 2:
            mixed = mixed * jax.nn.sigmoid(rows[2][...])
        x = x + _dot(mixed.astype(BF16), consts[3][...])
    xn = _rmsnorm(x, g_ref[...]).astype(BF16)
    acc = jnp.zeros_like(x)
    for c in range(d_ff // FFN_CHUNK):
        lo = c * FFN_CHUNK
        gate = _dot(xn, win_ref[:, lo:lo + FFN_CHUNK])
        up = _dot(xn, win_ref[:, d_ff + lo:d_ff + lo + FFN_CHUNK])
        act = (gate * jax.nn.sigmoid(gate) * up).astype(BF16)
        acc = acc + _dot(act, wout_ref[lo:lo + FFN_CHUNK, :])
    y = x + 0.5 * acc
    o_ref[...] = _rmsnorm(y, consts[-1][...]) if final else y


def _ffn(x, g, w_in, w_out, mixer=None, final_g=None):
    d = w_in.shape[0]
    rows, consts = [x], [g.reshape(1, d), w_in, w_out]
    if mixer is not None:
        rows += mixer[0]
        consts.append(mixer[1].astype(BF16))
    if final_g is not None:
        consts.append(final_g.reshape(1, d))
    body = functools.partial(_ffn_body, final=final_g is not None)
    return _row_call(body, rows, consts, [(d, F32, False)], "ffn")[0]


def _row_call(body, rows, consts, outs, name, tile_div=1):
    n_p, n_s = rows[0][0].shape[0], rows[0][1].shape[0]
    tile = ROW_TILE // tile_div
    assert n_p % tile == 0 and n_s % tile == 0
    tp, ts = n_p // tile, n_s // tile
    p_idx = lambda i: jnp.minimum(i, tp - 1)
    s_idx = lambda i: jnp.maximum(i - tp, 0)
    per_tile = tile // SUBLANES
    prev_of = lambda t: jnp.maximum(t * per_tile - 1, 0)
    in_specs, args, row_slots = [], [], []
    for entry in rows:
        prev = len(entry) == 3
        slots = []
        for a, idx in ((entry[0], p_idx), (entry[1], s_idx)):
            if a is None:
                slots.append(None)
                continue
            if prev:
                in_specs.append(pl.BlockSpec((SUBLANES, a.shape[1]), lambda i, idx=idx: (prev_of(idx(i)), 0)))
            else:
                in_specs.append(pl.BlockSpec((tile, a.shape[1]), lambda i, idx=idx: (idx(i), 0)))
            slots.append(len(args))
            args.append(a)
        row_slots.append(slots)
    n_row_args = len(args)
    for c in consts:
        in_specs.append(_const_spec(c.shape))
        args.append(c)
    out_specs, out_shape, out_slots = [], [], []
    for cols, dtype, kind in outs:
        slots = []
        if kind == "whole":
            out_specs.append(_const_spec(cols))
            out_shape.append(jax.ShapeDtypeStruct(cols, dtype))
            slots = [len(out_shape) - 1] * 2
        else:
            for n, idx, absent in ((n_p, p_idx, kind == "sample"), (n_s, s_idx, False)):
                if absent:
                    slots.append(None)
                    continue
                if kind is True:
                    out_specs.append(pl.BlockSpec((cols, tile), lambda i, idx=idx: (0, idx(i))))
                    out_shape.append(jax.ShapeDtypeStruct((cols, n), dtype))
                else:
                    out_specs.append(pl.BlockSpec((tile, cols), lambda i, idx=idx: (idx(i), 0)))
                    out_shape.append(jax.ShapeDtypeStruct((n, cols), dtype))
                slots.append(len(out_shape) - 1)
        out_slots.append(slots)
    nc = len(consts)

    def kern(*refs):
        i = pl.program_id(0)
        r, c, o = refs[:n_row_args], refs[n_row_args:n_row_args + nc], refs[n_row_args + nc:]
        pick = lambda seq, slots, side: [None if s[side] is None else seq[s[side]] for s in slots]

        @pl.when(i < tp)
        def _():
            body(pick(r, row_slots, 0), c, pick(o, out_slots, 0), False, i)

        @pl.when(i >= tp)
        def _():
            body(pick(r, row_slots, 1), c, pick(o, out_slots, 1), True, i - tp)

    res = pl.pallas_call(
        kern, grid=(tp + ts,), in_specs=in_specs, out_specs=out_specs, out_shape=out_shape,
        compiler_params=_params("arbitrary"), name=name,
    )(*args)
    result = []
    for (cols, dtype, kind), slots in zip(outs, out_slots):
        if kind == "whole":
            result.append(res[slots[0]])
        else:
            result.append(tuple(None if s is None else res[s] for s in slots))
    return result


def _split3(x):
    hi = x.astype(BF16)
    r1 = x - hi.astype(F32)
    mid = r1.astype(BF16)
    lo = (r1 - mid.astype(F32)).astype(BF16)
    return hi, mid, lo


def _dot01(m01, x):
    hi, mid, lo = _split3(x)
    return _dot(m01, hi) + _dot(m01, mid) + _dot(m01, lo)


def _dot01_r(x, m01):
    hi, mid, lo = _split3(x)
    return _dot(hi, m01) + _dot(mid, m01) + _dot(lo, m01)


def _log_sigmoid(x):
    return jnp.minimum(x, 0.0) - jnp.log1p(jnp.exp(-jnp.abs(x)))


FOX_HEADS = 16
FOX_HEAD_DIM = 64
FOX_WIDTH = FOX_HEADS * FOX_HEAD_DIM
FOX_SCALE = FOX_HEAD_DIM ** -0.5


def _fox_proj_body(rows, consts, outs, is_sample, tile_index):
    (x_ref,) = rows
    g_ref, w_ref, wf_ref, bf_ref = consts
    q_ref, k_ref, v_ref, gate_ref, f_ref, ft_ref = outs
    xn = _rmsnorm(x_ref[...], g_ref[...]).astype(BF16)
    for j, o_ref in enumerate((q_ref, k_ref, v_ref, gate_ref)):
        o_ref[...] = _dot(xn, w_ref[:, j * FOX_WIDTH:(j + 1) * FOX_WIDTH])
    logf = _log_sigmoid(_dot(xn, wf_ref[...]) + bf_ref[...])
    f_ref[...] = logf[:, :FOX_HEADS]
    ft_ref[...] = logf.T[:FOX_HEADS, :]


def _fox_project(x, g, w_in, b_f):
    w_main = w_in[:, :4 * FOX_WIDTH].astype(BF16)
    w_f = jnp.pad(w_in[:, 4 * FOX_WIDTH:], ((0, 0), (0, LANES - FOX_HEADS))).astype(BF16)
    b = jnp.pad(b_f, (0, LANES - FOX_HEADS)).reshape(1, LANES)
    outs = [(FOX_WIDTH, F32, False)] * 4 + [(FOX_HEADS, F32, False), (FOX_HEADS, F32, True)]
    return _row_call(_fox_proj_body, [x], [g.reshape(1, -1), w_main, w_f, b], outs, "fox_proj")


def _fox_cumsum_kernel(f_ref, ft_ref, d_ref, dt_ref, *, seq):
    r = lax.broadcasted_iota(jnp.int32, (LANES, LANES), 0)
    c = lax.broadcasted_iota(jnp.int32, (LANES, LANES), 1)
    lower = jnp.where(c <= r, 1.0, 0.0).astype(BF16)
    upper = jnp.where(r <= c, 1.0, 0.0).astype(BF16)
    carry_c = jnp.zeros((1, FOX_HEADS), F32)
    carry_r = jnp.zeros((FOX_HEADS, 1), F32)
    for blk in range(seq // LANES):
        sl = slice(blk * LANES, (blk + 1) * LANES)
        loc = _dot01(lower, f_ref[sl, :]) + carry_c
        d_ref[sl, :] = loc
        carry_c = loc[LANES - 1:LANES, :]
        loc_r = _dot01_r(ft_ref[:, sl], upper) + carry_r
        dt_ref[:, sl] = loc_r
        carry_r = loc_r[:, LANES - 1:LANES]


def _fox_cumsum(f, ft, batch, seq):
    return pl.pallas_call(
        functools.partial(_fox_cumsum_kernel, seq=seq),
        grid=(batch,),
        in_specs=[pl.BlockSpec((seq, FOX_HEADS), lambda b: (b, 0)),
                  pl.BlockSpec((FOX_HEADS, seq), lambda b: (0, b))],
        out_specs=[pl.BlockSpec((seq, FOX_HEADS), lambda b: (b, 0)),
                   pl.BlockSpec((FOX_HEADS, seq), lambda b: (0, b))],
        out_shape=[jax.ShapeDtypeStruct(f.shape, F32), jax.ShapeDtypeStruct(ft.shape, F32)],
        compiler_params=_params("parallel"),
        name="fox_cumsum",
    )(f, ft)


def _pair_head_norm(x, g, first_head):
    x2 = x * x
    s0 = jnp.sum(jnp.where(first_head, x2, 0.0), axis=-1, keepdims=True)
    s1 = jnp.sum(jnp.where(first_head, 0.0, x2), axis=-1, keepdims=True)
    inv = jnp.where(first_head, lax.rsqrt(s0 / FOX_HEAD_DIM + NORM_EPS), lax.rsqrt(s1 / FOX_HEAD_DIM + NORM_EPS))
    return x * inv * g


FOX_Q_BLOCK = 512


FOX_PAGE_GROUP = 16
FOX_PAGE_SUBGROUP = 4


def _fox_sample_kernel(pt_ref, q_ref, k_ref, v_ref, f_ref, gq_ref, gk_ref, *rest, group, n_groups, t_new):
    kp, vp, fp = rest[:group], rest[group:2 * group], rest[2 * group:3 * group]
    o_ref, kn_ref = rest[3 * group:3 * group + 2]
    q_sc, kn_sc, cq_sc, ck_sc, s_sc, m_sc, l_sc, acc_sc, carry_sc = rest[3 * group + 2:]
    step = pl.program_id(1)
    hd, rows_h = FOX_HEAD_DIM, SUBLANES
    rows, width = q_sc.shape
    page = kp[0].shape[3]
    row_head = lax.broadcasted_iota(jnp.int32, (rows, 1), 0) // rows_h

    @pl.when(step == 0)
    def _():
        own_cols = row_head == lax.broadcasted_iota(jnp.int32, (1, width), 1) // hd
        own_head = row_head == lax.broadcasted_iota(jnp.int32, (1, FOX_HEADS), 1)
        pad_rows = lambda x: jnp.concatenate([x, jnp.zeros((rows_h - t_new, x.shape[1]), F32)], axis=0)
        per_head = lambda x8: jnp.broadcast_to(x8[None], (FOX_HEADS,) + x8.shape).reshape(rows, x8.shape[1])

        def head_norm(x, g):
            xe = jnp.where(own_cols, per_head(pad_rows(x)), 0.0)
            ss = jnp.sum(xe * xe, axis=-1, keepdims=True)
            return xe * lax.rsqrt(ss / hd + NORM_EPS) * g

        q_sc[...] = head_norm(q_ref[0], gq_ref[...]) * FOX_SCALE
        kn = jnp.sum(head_norm(k_ref[0], gk_ref[...]).reshape(FOX_HEADS, rows_h, width), axis=0)
        kn_sc[...] = kn
        kn_ref[0] = kn[:t_new, :]
        f = f_ref[0]
        c_rows = [f[0:1, :]]
        for t in range(1, t_new):
            c_rows.append(c_rows[-1] + f[t:t + 1, :])
        c = jnp.concatenate(c_rows, axis=0)
        cq_sc[...] = jnp.sum(jnp.where(own_head, per_head(pad_rows(c)), 0.0), axis=-1, keepdims=True)
        for j in range(t_new):
            ck_sc[j] = jnp.sum(jnp.where(own_head, c[j:j + 1, :], 0.0), axis=-1, keepdims=True)
        m_sc[...] = jnp.full(m_sc.shape, -jnp.inf, F32)
        l_sc[...] = jnp.zeros(l_sc.shape, F32)
        acc_sc[...] = jnp.zeros(acc_sc.shape, F32)
        carry_sc[...] = jnp.zeros(carry_sc.shape, F32)

    r = lax.broadcasted_iota(jnp.int32, (page, page), 0)
    c_ = lax.broadcasted_iota(jnp.int32, (page, page), 1)
    later = jnp.where(r > c_, 1.0, 0.0).astype(BF16)
    qb = q_sc[...].astype(BF16)
    cq = cq_sc[...]
    carry = carry_sc[...]
    for g in reversed(range(group)):
        ft = fp[g][0]
        suffix = _dot01_r(ft, later) + carry
        carry = carry + jnp.sum(ft, axis=-1, keepdims=True)
        bias = jnp.broadcast_to(suffix[:, None, :], (FOX_HEADS, rows_h, page)).reshape(rows, page)
        kb = kp[g][0].reshape(width, page).astype(BF16)
        s_sc[:, g * page:(g + 1) * page] = _dot(qb, kb) + bias + cq
    carry_sc[...] = carry

    m_run, l_run = m_sc[...], l_sc[...]
    acc = acc_sc[...]
    for first in range(0, group, FOX_PAGE_SUBGROUP):
        pages = range(first, min(first + FOX_PAGE_SUBGROUP, group))
        s_sub = s_sc[:, pages[0] * page:(pages[-1] + 1) * page]
        m_new = jnp.maximum(m_run, jnp.max(s_sub, axis=-1, keepdims=True))
        alpha = jnp.exp(m_run - m_new)
        p_sub = jnp.exp(s_sub - m_new)
        l_run = alpha * l_run + jnp.sum(p_sub, axis=-1, keepdims=True)
        m_run = m_new
        vt = jnp.concatenate([vp[g][0].reshape(width, page).astype(BF16) for g in pages], axis=1)
        acc = alpha * acc + _dot_nt(p_sub.astype(BF16), vt)
    m_sc[...] = m_run
    l_sc[...] = l_run
    acc_sc[...] = acc

    @pl.when(step == n_groups - 1)
    def _():
        t_row = lax.broadcasted_iota(jnp.int32, (rows, 1), 0) % rows_h
        own_cols = row_head == lax.broadcasted_iota(jnp.int32, (1, width), 1) // hd
        q_all = q_sc[...]
        s_new = []
        for j in range(t_new):
            s_j = jnp.sum(q_all * kn_sc[j:j + 1, :], axis=-1, keepdims=True) + cq - ck_sc[j]
            s_new.append(jnp.where(t_row >= j, s_j, -jnp.inf))
        m_new = m_run
        for s_j in s_new:
            m_new = jnp.maximum(m_new, s_j)
        alpha = jnp.exp(m_run - m_new)
        l = alpha * l_run
        o = alpha * acc
        v = v_ref[0]
        for j, s_j in enumerate(s_new):
            p_j = jnp.exp(s_j - m_new)
            l = l + p_j
            o = o + p_j * v[j:j + 1, :]
        o = jnp.where(own_cols, o / l, 0.0)
        o_ref[0] = jnp.sum(o.reshape(FOX_HEADS, rows_h, width), axis=0)[:t_new, :]


def _fox_sample_attention(q, k, v, f, cache_kt, cache_vt, cache_ft, page_table, gq, gk):
    batch, t_new, width = q.shape
    n_pages = page_table.shape[1]
    page = cache_kt.shape[3]
    group = FOX_PAGE_GROUP
    n_groups = n_pages // group
    rows = FOX_HEADS * SUBLANES

    def page_spec(shape, g):
        def index(b, s, pt):
            return (pt[b * n_pages + (n_groups - 1 - s) * group + g],) + (0,) * (len(shape) - 1)
        return pl.BlockSpec((1,) + shape[1:], index)

    tok_spec = lambda last: pl.BlockSpec((1, t_new, last), lambda b, s, pt: (b, 0, 0))
    in_specs = [tok_spec(width), tok_spec(width), tok_spec(width), tok_spec(FOX_HEADS),
                pl.BlockSpec((1, width), lambda b, s, pt: (0, 0)),
                pl.BlockSpec((1, width), lambda b, s, pt: (0, 0))]
    in_specs += [page_spec(cache_kt.shape, g) for g in range(group)]
    in_specs += [page_spec(cache_vt.shape, g) for g in range(group)]
    in_specs += [page_spec(cache_ft.shape, g) for g in range(group)]
    grid_spec = pltpu.PrefetchScalarGridSpec(
        num_scalar_prefetch=1,
        grid=(batch, n_groups),
        in_specs=in_specs,
        out_specs=[tok_spec(width), tok_spec(width)],
        scratch_shapes=[
            pltpu.VMEM((rows, width), F32),
            pltpu.VMEM((SUBLANES, width), F32),
            pltpu.VMEM((rows, 1), F32),
            pltpu.VMEM((t_new, rows, 1), F32),
            pltpu.VMEM((rows, group * page), F32),
            pltpu.VMEM((rows, 1), F32),
            pltpu.VMEM((rows, 1), F32),
            pltpu.VMEM((rows, width), F32),
            pltpu.VMEM((FOX_HEADS, 1), F32),
        ],
    )
    heads_tiled = lambda g: jnp.tile(g, FOX_HEADS).reshape(1, width)
    return pl.pallas_call(
        functools.partial(_fox_sample_kernel, group=group, n_groups=n_groups, t_new=t_new),
        grid_spec=grid_spec,
        out_shape=[jax.ShapeDtypeStruct(q.shape, F32), jax.ShapeDtypeStruct(q.shape, F32)],
        compiler_params=_params("parallel", "arbitrary"),
        name="fox_sample_attention",
    )(page_table.reshape(-1), q, k, v, f, heads_tiled(gq), heads_tiled(gk),
      *([cache_kt] * group), *([cache_vt] * group), *([cache_ft] * group))


LOG2E = 1.4426950408889634
FOX_BIAS_ROWS = 16


def _fox_prompt_kernel(q_ref, k_ref, v_ref, d_ref, dt_ref, gq_ref, gk_ref, o_ref, knt_ref, vt_ref, kt_sc, v_sc, *, tq):
    pair = pl.program_id(1)
    qi = pl.program_id(2)
    first_head = lax.broadcasted_iota(jnp.int32, (1, LANES), 1) < FOX_HEAD_DIM
    n_blk = kt_sc.shape[0]

    @pl.when(qi == 0)
    def _():
        knt = _pair_head_norm(k_ref[...], gk_ref[...], first_head).T
        seq = knt.shape[1]
        knt_ref[0] = knt.reshape(2, FOX_HEAD_DIM, seq)
        sub = lax.broadcasted_iota(jnp.int32, (FOX_BIAS_ROWS, 1), 0)
        bias_rows = jnp.where(sub < 3, 1.0, 0.0).astype(BF16) * jnp.ones((1, seq), BF16)
        for e in (0, 1):
            parts = _split3(dt_ref[pl.ds(2 * pair + e, 1), :] * (-LOG2E))
            for i, part in enumerate(parts):
                bias_rows = jnp.where(sub == 3 + 3 * e + i, part, bias_rows)
        for j in range(n_blk):
            cols = slice(j * tq, (j + 1) * tq)
            kt_sc[j, :LANES, :] = knt[:, cols].astype(BF16)
            kt_sc[j, LANES:, :] = bias_rows[:, cols]
        v = v_ref[...]
        v_sc[...] = v.astype(BF16)
        vt = v.T
        vt_ref[0] = vt.reshape(2, FOX_HEAD_DIM, seq)

    qn = _pair_head_norm(q_ref[...], gq_ref[...], first_head) * (FOX_SCALE * LOG2E)
    d_blk = d_ref[...]
    head_lane = lax.broadcasted_iota(jnp.int32, (1, FOX_HEADS), 1)
    lane_b = lax.broadcasted_iota(jnp.int32, (1, FOX_BIAS_ROWS), 1)
    q_heads = []
    for e in (0, 1):
        d_q = jnp.sum(jnp.where(head_lane == 2 * pair + e, d_blk, 0.0), axis=-1, keepdims=True) * LOG2E
        ones_at = (lane_b >= 3 + 3 * e) & (lane_b < 6 + 3 * e)
        bias_cols = jnp.where(ones_at, 1.0, 0.0).astype(BF16) * jnp.ones((tq, 1), BF16)
        for i, part in enumerate(_split3(d_q)):
            bias_cols = jnp.where(lane_b == i, part, bias_cols)
        mine = first_head if e == 0 else jnp.logical_not(first_head)
        q_heads.append(jnp.concatenate([jnp.where(mine, qn, 0.0).astype(BF16), bias_cols], axis=1))
    row = lax.broadcasted_iota(jnp.int32, (tq, tq), 0)
    col = lax.broadcasted_iota(jnp.int32, (tq, tq), 1)

    def attend(n_kv):
        scores = lambda j: [_dot(q_heads[e], kt_sc[j]) for e in (0, 1)]
        carry = [(jnp.full((tq, 1), -jnp.inf, F32), jnp.zeros((tq, 1), F32), jnp.zeros((tq, LANES), F32))
                 for _ in (0, 1)]
        s_cur = scores(0)
        for j in range(n_kv):
            s_next = scores(j + 1) if j + 1 < n_kv else None
            vb = v_sc[j * tq:(j + 1) * tq, :]
            for e in (0, 1):
                m, l, acc = carry[e]
                s = s_cur[e]
                if j == n_kv - 1:
                    s = jnp.where(col <= row, s, -jnp.inf)
                m_new = jnp.maximum(m, jnp.max(s, axis=-1, keepdims=True))
                alpha = jnp.exp2(m - m_new)
                p = jnp.exp2(s - m_new)
                l = alpha * l + jnp.sum(p, axis=-1, keepdims=True)
                acc = alpha * acc + _dot(p.astype(BF16), vb)
                carry[e] = (m_new, l, acc)
            s_cur = s_next
        (_, l0, acc0), (_, l1, acc1) = carry
        return jnp.where(first_head, acc0 / l0, acc1 / l1)

    for n_kv in range(1, n_blk + 1):
        @pl.when(qi == n_kv - 1)
        def _(n_kv=n_kv):
            o_ref[...] = attend(n_kv)


def _fox_prompt_attention(q, k, v, dcum, dcum_t, gq, gk, batch, seq):
    tq = FOX_Q_BLOCK
    nq = seq // tq
    n = batch * seq
    pairs = FOX_WIDTH // LANES
    gq2 = jnp.tile(gq, 2).reshape(1, LANES)
    gk2 = jnp.tile(gk, 2).reshape(1, LANES)
    t_shape = jax.ShapeDtypeStruct((batch, FOX_HEADS, FOX_HEAD_DIM, seq), F32)
    t_spec = pl.BlockSpec((1, 2, FOX_HEAD_DIM, seq), lambda b, p, i: (b, p, 0, 0))
    return pl.pallas_call(
        functools.partial(_fox_prompt_kernel, tq=tq),
        grid=(batch, pairs, nq),
        in_specs=[
            pl.BlockSpec((tq, LANES), lambda b, p, i: (b * nq + i, p)),
            pl.BlockSpec((seq, LANES), lambda b, p, i: (b, p)),
            pl.BlockSpec((seq, LANES), lambda b, p, i: (b, p)),
            pl.BlockSpec((tq, FOX_HEADS), lambda b, p, i: (b * nq + i, 0)),
            pl.BlockSpec((FOX_HEADS, seq), lambda b, p, i: (0, b)),
            _const_spec((1, LANES)),
            _const_spec((1, LANES)),
        ],
        out_specs=[pl.BlockSpec((tq, LANES), lambda b, p, i: (b * nq + i, p)), t_spec, t_spec],
        out_shape=[jax.ShapeDtypeStruct((n, FOX_WIDTH), F32), t_shape, t_shape],
        scratch_shapes=[pltpu.VMEM((nq, LANES + FOX_BIAS_ROWS, tq), BF16), pltpu.VMEM((seq, LANES), BF16)],
        compiler_params=_params("parallel", "parallel", "arbitrary"),
        name="fox_prompt_attention",
    )(q, k, v, dcum, dcum_t, gq2, gk2)


def _fox_layer(x, g, w_in, b_f, gq, gk, w_out, cache_k, cache_v, cache_f, page_table, dims):
    batch, seq, dec_batch, dec_seq = dims
    q, k, v, gate, f, ft = _fox_project(x, g, w_in, b_f)
    dcum, dcum_t = _fox_cumsum(f[0], ft[0], batch, seq)
    o_p, knt_p, vt_p = _fox_prompt_attention(q[0], k[0], v[0], dcum, dcum_t, gq, gk, batch, seq)
    per_seq = lambda a: a.reshape(dec_batch, dec_seq, a.shape[-1])
    o_s, kn_s = _fox_sample_attention(per_seq(q[1]), per_seq(k[1]), per_seq(v[1]), per_seq(f[1]),
                                      jnp.transpose(cache_k, (0, 2, 3, 1)), jnp.transpose(cache_v, (0, 2, 3, 1)),
                                      jnp.swapaxes(cache_f, 1, 2), page_table, gq, gk)
    o = (o_p, o_s.reshape(-1, FOX_WIDTH))
    heads = lambda a, b, t: a.reshape(b, t, FOX_HEADS, FOX_HEAD_DIM)
    seq_first = lambda a: jnp.transpose(a, (0, 3, 1, 2))
    state = (seq_first(knt_p), seq_first(vt_p), f[0].reshape(batch, seq, FOX_HEADS),
             heads(kn_s, dec_batch, dec_seq), heads(v[1], dec_batch, dec_seq),
             f[1].reshape(dec_batch, dec_seq, FOX_HEADS))
    return ([o, gate], w_out), state


RWKV_HEAD_DIM = 64
RWKV_GN_EPS = 64e-5
RWKV_GATE_LORA_PAD = 256
RWKV_CHUNK = 128
RWKV_CHUNKS_PER_STEP = 8


def _rwkv_proj_body(rows, consts, outs, is_sample, tile_index, *, seq, dec_seq):
    x_ref, prev_ref, start_ref = rows
    (g_ref, mu_ref, w_rkv_ref, w0_ref, w1_ref, w2_ref, a0_ref, a1_ref, a2_ref, g1_ref, g2_ref) = consts
    r_ref, k_ref, v_ref, ld_ref, a_ref, gate_ref, xn_ref, last_ref = outs
    tile = x_ref.shape[0]
    xn = _rmsnorm(x_ref[...], g_ref[...])
    rolled = pltpu.roll(xn, 1, 0)
    row = lax.broadcasted_iota(jnp.int32, (tile, 1), 0)
    if is_sample:
        xn_ref[...] = xn
        x_prev = jnp.where(row % dec_seq == 0, start_ref[...], rolled)
    else:
        before = _rmsnorm(prev_ref[SUBLANES - 1:SUBLANES, :], g_ref[...])
        before = jnp.where((tile_index * tile) % seq == 0, 0.0, before)
        x_prev = jnp.where(row == 0, before, rolled)

        @pl.when(((tile_index + 1) * tile) % seq == 0)
        def _():
            last_ref[pl.ds((tile_index * tile) // seq, 1), :] = xn[tile - 1:tile, :]

    xx = x_prev - xn
    mix = lambda i: (xn + xx * mu_ref[i:i + 1, :]).astype(BF16)
    r_ref[...] = _dot(mix(0), w_rkv_ref[0])
    k_ref[...] = _dot(mix(2), w_rkv_ref[1])
    v_ref[...] = _dot(mix(3), w_rkv_ref[2])
    wl = w0_ref[...] + _dot(jnp.tanh(_dot(mix(1), w1_ref[...])).astype(BF16), w2_ref[...])
    softplus_neg = jnp.maximum(-wl, 0.0) + jnp.log1p(jnp.exp(-jnp.abs(wl)))
    ld_ref[...] = -jnp.exp(-softplus_neg - 0.5)
    a_ref[...] = jax.nn.sigmoid(a0_ref[...] + _dot(_dot(mix(4), a1_ref[...]).astype(BF16), a2_ref[...]))
    gate_ref[...] = _dot(jax.nn.sigmoid(_dot(mix(5), g1_ref[...])).astype(BF16), g2_ref[...])


def _dot3(a, b):
    a_hi = a.astype(BF16)
    a_lo = (a - a_hi.astype(F32)).astype(BF16)
    b_hi = b.astype(BF16)
    b_lo = (b - b_hi.astype(F32)).astype(BF16)
    return _dot(a_hi, b_hi) + _dot(a_hi, b_lo) + _dot(a_lo, b_hi)


def _pair_sums(x, first_head):
    s0 = jnp.sum(jnp.where(first_head, x, 0.0), axis=-1, keepdims=True)
    s1 = jnp.sum(jnp.where(first_head, 0.0, x), axis=-1, keepdims=True)
    return jnp.where(first_head, s0, s1)


def _rwkv_chunk_terms(chunks, kk_w, ka_w):
    length = chunks[0][0].shape[0]
    first_head = lax.broadcasted_iota(jnp.int32, (1, LANES), 1) < RWKV_HEAD_DIM
    head_masks = (first_head, jnp.logical_not(first_head))
    ti = lax.broadcasted_iota(jnp.int32, (length, length), 0)
    si = lax.broadcasted_iota(jnp.int32, (length, length), 1)
    strict, incl = si < ti, si <= ti
    lower01 = jnp.where(incl, 1.0, 0.0).astype(BF16)
    levels = max(1, math.ceil(math.log2(length)))
    insts = [(c, e) for c in range(len(chunks)) for e in (0, 1)]

    cl = [_dot01(lower01, ld) for (_, _, _, ld, _) in chunks]
    pre = []
    for (r, k, v, ld, a), cl_c in zip(chunks, cl):
        kk = k * kk_w
        kkn = kk / jnp.maximum(jnp.sqrt(_pair_sums(kk * kk, first_head)), 1e-12)
        k2 = k * (1.0 + (a - 1.0) * ka_w)
        aa, bb = -kkn, kkn * a
        cl_last = cl_c[length - 1:length, :]
        cl_mid = cl_c[length // 2 - 1:length // 2, :]
        p_out = jnp.exp(cl_mid - cl_c)
        p_in = jnp.exp(-cl_mid)
        p_end = jnp.exp(cl_last - cl_c)
        r_t, a_t = r * jnp.exp(cl_c), aa * jnp.exp(cl_c - ld)
        pre.append(dict(
            k2=k2, vb=v.astype(BF16), r_t=r_t, a_t=a_t,
            lhs=jnp.concatenate([a_t * p_in, r_t * p_in], axis=0),
            rhs=jnp.concatenate([bb * p_out, k2 * p_out], axis=0).astype(BF16),
            b_end=(bb * p_end).astype(BF16), k_end=(k2 * p_end).astype(BF16), decay_end=jnp.exp(cl_last)))
    a_e = [jnp.where(head_masks[e], pre[c]["a_t"], 0.0) for c, e in insts]
    r_e = [jnp.where(head_masks[e], pre[c]["r_t"], 0.0) for c, e in insts]
    mm = [_dot_nt(jnp.where(head_masks[e], pre[c]["lhs"], 0.0).astype(BF16), pre[c]["rhs"])
          for c, e in insts]
    m_ak = [jnp.where(strict, m[:length, length:], 0.0).astype(BF16) for m in mm]
    m_rb = [jnp.where(incl, m[length:, :length], 0.0).astype(BF16) for m in mm]
    m_rk = [jnp.where(incl, m[length:, length:], 0.0).astype(BF16) for m in mm]
    power = [jnp.where(strict, m[:length, :length], 0.0) for m in mm]
    x = [jnp.concatenate([a_e[i], _dot(m_ak[i], pre[c]["vb"])], axis=1) for i, (c, e) in enumerate(insts)]
    for lvl in range(levels):
        pb = [p_i.astype(BF16) for p_i in power]
        x = [x_i + _dot(p_i, x_i.astype(BF16)) for x_i, p_i in zip(x, pb)]
        if lvl + 1 < levels:
            power = [_dot(p_i, p_i) for p_i in pb]
    w_e = [x_i[:, :LANES] for x_i in x]
    u0_e = [x_i[:, LANES:] for x_i in x]
    yr_e = [r_e[i] + _dot(m_rb[i], w_e[i].astype(BF16)) for i in range(len(insts))]
    y0_e = [_dot(m_rb[i], u0_e[i].astype(BF16)) + _dot(m_rk[i], pre[c]["vb"]) for i, (c, e) in enumerate(insts)]
    row = lax.broadcasted_iota(jnp.int32, (LANES, LANES), 0)
    col = lax.broadcasted_iota(jnp.int32, (LANES, LANES), 1)
    same_head = (row < RWKV_HEAD_DIM) == (col < RWKV_HEAD_DIM)
    out = []
    for c in range(len(chunks)):
        i0, i1 = 2 * c, 2 * c + 1
        w = (w_e[i0] + w_e[i1]).astype(BF16)
        u0 = jnp.where(first_head, u0_e[i0], u0_e[i1]).astype(BF16)
        g = jnp.where(same_head, jnp.where(row == col, pre[c]["decay_end"], 0.0) + _dot_tn(pre[c]["b_end"], w), 0.0)
        h = jnp.where(same_head, _dot_tn(pre[c]["b_end"], u0) + _dot_tn(pre[c]["k_end"], pre[c]["vb"]), 0.0)
        out.append((g, h, yr_e[i0] + yr_e[i1], jnp.where(first_head, y0_e[i0], y0_e[i1]), pre[c]["k2"]))
    return out


def _rwkv_finish(y, r, k2, v, gate, rk_w, gnw, gnb):
    first_head = lax.broadcasted_iota(jnp.int32, (1, LANES), 1) < RWKV_HEAD_DIM
    mean = _pair_sums(y, first_head) / RWKV_HEAD_DIM
    yc = y - mean
    var = _pair_sums(yc * yc, first_head) / RWKV_HEAD_DIM
    yn = yc * lax.rsqrt(var + RWKV_GN_EPS) * gnw + gnb
    bonus = _pair_sums(r * k2 * rk_w, first_head) * v
    return (yn + bonus) * gate


def _rwkv_scan_kernel(r_ref, k_ref, v_ref, ld_ref, a_ref, gate_ref, kk_ref, ka_ref, rk_ref, gnw_ref, gnb_ref,
                      z_ref, s_ref, state_sc, *, length, n_inst):
    step = pl.program_id(2)

    @pl.when(step == 0)
    def _():
        state_sc[...] = jnp.zeros(state_sc.shape, F32)

    state = state_sc[...]
    rows = [slice(i * length, (i + 1) * length) for i in range(n_inst)]
    chunks = [(r_ref[rs, :], k_ref[rs, :], v_ref[rs, :], ld_ref[rs, :], a_ref[rs, :]) for rs in rows]
    terms = _rwkv_chunk_terms(chunks, kk_ref[...], ka_ref[...])
    for rs, (r, k, v, ld, a), (g, h, yr, y0, k2) in zip(rows, chunks, terms):
        y = _dot(yr.astype(BF16), state.astype(BF16)) + y0
        state = _dot3(g, state) + h
        z_ref[rs, :] = _rwkv_finish(y, r, k2, v, gate_ref[rs, :], rk_ref[...], gnw_ref[...], gnb_ref[...])
    state_sc[...] = state

    @pl.when(step == pl.num_programs(2) - 1)
    def _():
        st = state.T
        s_ref[0, 0] = st[:RWKV_HEAD_DIM, :RWKV_HEAD_DIM]
        s_ref[0, 1] = st[RWKV_HEAD_DIM:, RWKV_HEAD_DIM:]


def _rwkv_scan(r, k, v, ld, a, gate, params, n_seq, seq):
    d = r.shape[1]
    pairs = d // LANES
    heads = d // RWKV_HEAD_DIM
    rows = RWKV_CHUNKS_PER_STEP * RWKV_CHUNK
    steps = seq // rows
    row_spec = pl.BlockSpec((rows, LANES), lambda b, p, s: (b * steps + s, p))
    par_spec = pl.BlockSpec((1, LANES), lambda b, p, s: (0, p))
    st_spec = pl.BlockSpec((1, 2, RWKV_HEAD_DIM, RWKV_HEAD_DIM), lambda b, p, s: (b, p, 0, 0))
    return pl.pallas_call(
        functools.partial(_rwkv_scan_kernel, length=RWKV_CHUNK, n_inst=RWKV_CHUNKS_PER_STEP),
        grid=(n_seq, pairs, steps),
        in_specs=[row_spec] * 6 + [par_spec] * 5,
        out_specs=[row_spec, st_spec],
        out_shape=[jax.ShapeDtypeStruct(r.shape, F32),
                   jax.ShapeDtypeStruct((n_seq, heads, RWKV_HEAD_DIM, RWKV_HEAD_DIM), F32)],
        scratch_shapes=[pltpu.VMEM((LANES, LANES), F32)],
        compiler_params=_params("parallel", "parallel", "arbitrary"),
        name="rwkv_scan_prompt",
    )(r, k, v, ld, a, gate, *params)


def _rwkv_sample_kernel(r_ref, k_ref, v_ref, ld_ref, a_ref, gate_ref, kk_ref, ka_ref, rk_ref, gnw_ref, gnb_ref,
                        s0_ref, z_ref, s_ref, y_sc, *, steps):
    n = RWKV_HEAD_DIM
    s_ref[...] = s0_ref[...]
    kk_w, ka_w, rk_w, gnw, gnb = kk_ref[0], ka_ref[0], rk_ref[0], gnw_ref[0], gnb_ref[0]
    for t in range(steps):
        r, k, v, a = r_ref[t, 0], k_ref[t, 0], v_ref[t, 0], a_ref[t, 0]
        decay = jnp.exp(ld_ref[t, 0])
        kk = k * kk_w
        kkn = kk / jnp.maximum(jnp.sqrt(jnp.sum(kk * kk, axis=0, keepdims=True)), 1e-12)
        k2 = k * (1.0 + (a - 1.0) * ka_w)
        aa, bb = -kkn, kkn * a

        def value_row(i, carry):
            s_i = s_ref[0, i]
            sa = jnp.sum(s_i * aa, axis=0, keepdims=True)
            v_i = v_ref[t, 0, pl.ds(i, 1), :]
            s_i = s_i * decay + sa * bb + v_i * k2
            s_ref[0, i] = s_i
            y_sc[pl.ds(i, 1), :] = jnp.sum(s_i * r, axis=0, keepdims=True)
            return carry

        lax.fori_loop(0, n, value_row, 0, unroll=4)
        y = y_sc[...]
        mean = jnp.mean(y, axis=0, keepdims=True)
        yc = y - mean
        var = jnp.mean(yc * yc, axis=0, keepdims=True)
        yn = yc * lax.rsqrt(var + RWKV_GN_EPS) * gnw + gnb
        bonus = jnp.sum(r * k2 * rk_w, axis=0, keepdims=True) * v
        z_ref[t, 0] = (yn + bonus) * gate_ref[t, 0]


def _rwkv_sample(r, k, v, ld, a, gate, s0, params, dec_batch, dec_seq):
    d = r.shape[1]
    n = RWKV_HEAD_DIM
    heads = d // n
    lanes_last = lambda t: jnp.transpose(t.reshape(dec_batch, dec_seq, heads, n), (1, 2, 3, 0))
    tok_spec = pl.BlockSpec((dec_seq, 1, n, dec_batch), lambda h: (0, h, 0, 0))
    par_spec = pl.BlockSpec((1, n, 1), lambda h: (h, 0, 0))
    st_spec = pl.BlockSpec((1, n, n, dec_batch), lambda h: (h, 0, 0, 0))
    z, s = pl.pallas_call(
        functools.partial(_rwkv_sample_kernel, steps=dec_seq),
        grid=(heads,),
        in_specs=[tok_spec] * 6 + [par_spec] * 5 + [st_spec],
        out_specs=[tok_spec, st_spec],
        out_shape=[jax.ShapeDtypeStruct((dec_seq, heads, n, dec_batch), F32),
                   jax.ShapeDtypeStruct((heads, n, n, dec_batch), F32)],
        scratch_shapes=[pltpu.VMEM((n, dec_batch), F32)],
        compiler_params=_params("parallel"),
        name="rwkv_scan_sample",
    )(*[lanes_last(t) for t in (r, k, v, ld, a, gate)], *[p.reshape(heads, n, 1) for p in params],
      jnp.transpose(s0, (1, 2, 3, 0)))
    return jnp.transpose(z, (3, 0, 1, 2)).reshape(dec_batch * dec_seq, d), jnp.transpose(s, (3, 0, 1, 2))


def _rwkv_layer(x, g, shift_state, wkv_state, mu, w0, w1, w2, a0, a1, a2, g1, g2, kk_w, ka_w, rk_w, w_rkv, w_o,
                gn_w, gn_b, dims):
    batch, seq, dec_batch, dec_seq = dims
    d = mu.shape[1]
    heads = d // RWKV_HEAD_DIM
    row = lambda t: t.reshape(1, d)
    starts = jnp.concatenate([shift_state[:, None, :], jnp.zeros((dec_batch, dec_seq - 1, d), F32)], axis=1)
    rank = g1.shape[1]
    g1p = jnp.pad(g1, ((0, 0), (0, RWKV_GATE_LORA_PAD - rank))).astype(BF16)
    g2p = jnp.pad(g2, ((0, RWKV_GATE_LORA_PAD - rank), (0, 0))).astype(BF16)
    consts = [row(g), mu, w_rkv.astype(BF16), row(w0), w1.astype(BF16), w2.astype(BF16), row(a0), a1.astype(BF16),
              a2.astype(BF16), g1p, g2p]
    outs = [(d, F32, False)] * 6 + [(d, F32, "sample"), ((batch, d), F32, "whole")]
    r, k, v, ld, a, gate, xn, last_p = _row_call(
        functools.partial(_rwkv_proj_body, seq=seq, dec_seq=dec_seq),
        [x, (x[0], None, "prev"), (None, starts.reshape(-1, d))], consts, outs, "rwkv_proj", tile_div=2)
    params = [row(t) for t in (kk_w, ka_w, rk_w.reshape(-1), gn_w, gn_b)]
    z_p, s_p = _rwkv_scan(r[0], k[0], v[0], ld[0], a[0], gate[0], params, batch, seq)
    z_s, s_s = _rwkv_sample(r[1], k[1], v[1], ld[1], a[1], gate[1], wkv_state, params, dec_batch, dec_seq)
    shift_s = xn[1].reshape(dec_batch, dec_seq, d)[:, dec_seq - 1]
    return ([(z_p, z_s)], w_o), (last_p, s_p, shift_s, s_s)


MLSTM_HEADS = 4
MLSTM_QK_DIM = 128
MLSTM_V_DIM = 256
MLSTM_CHUNK = 64
MLSTM_GATE_CAP = 15.0
MLSTM_CHUNKS_PER_STEP = 8
MLSTM_SEQS_PER_STEP = 16
MLSTM_NO_INPUT = -1e30


def _mlstm_proj_body(rows, consts, outs, is_sample, tile_index):
    (x_ref,), (g_ref, w_ref, wg_ref, bg_ref) = rows, consts
    q_ref, k_ref, v_ref, og_ref, ig_ref, igt_ref = outs
    hk = MLSTM_HEADS * MLSTM_QK_DIM
    hv = MLSTM_HEADS * MLSTM_V_DIM
    xn = _rmsnorm(x_ref[...], g_ref[...]).astype(BF16)
    q_ref[...] = _dot(xn, w_ref[:, :hk]) * MLSTM_QK_DIM ** -0.5
    k_ref[...] = _dot(xn, w_ref[:, hk:2 * hk])
    v_ref[...] = _dot(xn, w_ref[:, 2 * hk:2 * hk + hv])
    og_ref[...] = _dot(xn, w_ref[:, 2 * hk + hv:])
    gates = MLSTM_GATE_CAP * jnp.tanh((_dot(xn, wg_ref[...]) + bg_ref[...]) / MLSTM_GATE_CAP)
    is_forget = lax.broadcasted_iota(jnp.int32, (1, LANES), 1) >= MLSTM_HEADS
    ig = jnp.where(is_forget, _log_sigmoid(gates), gates)
    ig_ref[...] = ig[:, :2 * MLSTM_HEADS]
    igt_ref[...] = ig.T[:2 * MLSTM_HEADS, :]


def _mlstm_chunks(chunks, states, head, gn, chained):
    length = chunks[0][0].shape[0]
    nh = MLSTM_HEADS
    count = len(chunks)
    ti = lax.broadcasted_iota(jnp.int32, (length, length), 0)
    si = lax.broadcasted_iota(jnp.int32, (length, length), 1)
    causal = si <= ti
    lower01 = jnp.where(causal, 1.0, 0.0).astype(BF16)
    upper01 = jnp.where(ti <= si, 1.0, 0.0).astype(BF16)
    lane_g = lax.broadcasted_iota(jnp.int32, (1, 2 * nh), 1)
    sub_g = lax.broadcasted_iota(jnp.int32, (2 * nh, 1), 0)
    col = lambda x, j: jnp.sum(jnp.where(lane_g == j, x, 0.0), axis=-1, keepdims=True)
    rowv = lambda x, j: jnp.sum(jnp.where(sub_g == j, x, 0.0), axis=0, keepdims=True)
    last = slice(length - 1, length)

    cum_c = [_dot01(lower01, ch[4]) for ch in chunks]
    cum_r = [_dot01_r(ch[5], upper01) for ch in chunks]
    b_c = [col(x, nh + head) for x in cum_c]
    li_c = [col(ch[4], head) for ch in chunks]
    logw = [jnp.where(causal, b - rowv(xr, nh + head) + rowv(ch[5], head), -jnp.inf)
            for b, xr, ch in zip(b_c, cum_r, chunks)]
    max_logw = [jnp.max(x, axis=-1, keepdims=True) for x in logw]
    qb = [ch[0].astype(BF16) for ch in chunks]
    kb = [ch[1].astype(BF16) for ch in chunks]
    vb = [ch[2].astype(BF16) for ch in chunks]
    qk = [_dot_nt(a, b) for a, b in zip(qb, kb)]
    m_in, m_t = [], []
    for i in range(count):
        m_prev = states[i][2] if (not chained or i == 0) else m_t[i - 1][last, :]
        m_in.append(m_prev)
        m_t.append(jnp.maximum(m_prev + b_c[i], max_logw[i]))
    inter = [m + b for m, b in zip(m_in, b_c)]
    inter_scale = [jnp.exp(a - b) for a, b in zip(inter, m_t)]
    sw = [jnp.exp(lw - mt) * s for lw, mt, s in zip(logw, m_t, qk)]
    swv = [_dot(s.astype(BF16), v) for s, v in zip(sw, vb)]
    m_new = [mt[last, :] for mt in m_t]
    carry_scale = [jnp.exp(a[last, :] - b) for a, b in zip(inter, m_new)]
    w_s = [jnp.exp(b[last, :] - b + li - mn) for b, li, mn in zip(b_c, li_c, m_new)]
    kv = [_dot_tn((ch[2] * w).astype(BF16), k) for ch, w, k in zip(chunks, w_s, kb)]
    ksum = [jnp.sum(w * ch[1], axis=0, keepdims=True) for ch, w in zip(chunks, w_s)]
    c_in, n_in, after = [], [], []
    for i in range(count):
        c_prev, n_prev = (states[i][0], states[i][1]) if (not chained or i == 0) else after[i - 1][:2]
        c_in.append(c_prev)
        n_in.append(n_prev)
        after.append((carry_scale[i] * c_prev + kv[i], carry_scale[i] * n_prev + ksum[i], m_new[i]))
    outs = []
    for i, ch in enumerate(chunks):
        num = inter_scale[i] * _dot_nt(qb[i], c_in[i].astype(BF16)) + swv[i]
        den = inter_scale[i] * jnp.sum(ch[0] * n_in[i], axis=-1, keepdims=True) + jnp.sum(sw[i], axis=-1, keepdims=True)
        h = num / jnp.maximum(jnp.abs(den), jnp.exp(-m_t[i]))
        h = h * lax.rsqrt(jnp.mean(h * h, axis=-1, keepdims=True) + NORM_EPS)
        outs.append(h * gn * jax.nn.sigmoid(ch[3]))
    return outs, after


def _mlstm_scan_kernel(q_ref, k_ref, v_ref, og_ref, ig_ref, igt_ref, gn_ref, c0_ref, nm0_ref,
                       h_ref, c_ref, nm_ref, c_sc, nm_sc, *, length, n_inst, chained):
    head = pl.program_id(1)
    step = pl.program_id(2)
    rows = [slice(i * length, (i + 1) * length) for i in range(n_inst)]
    chunks = [(q_ref[rs, :], k_ref[rs, :], v_ref[rs, :], og_ref[rs, :], ig_ref[rs, :], igt_ref[:, rs]) for rs in rows]
    pack_nm = lambda n, m: jnp.concatenate([n, jnp.broadcast_to(m, n.shape)], axis=0)
    if chained:
        @pl.when(step == 0)
        def _():
            c_sc[...] = c0_ref[0, 0]
            nm_sc[...] = nm0_ref[0, 0]
        states = [(c_sc[...], nm_sc[0:1, :], nm_sc[1:2, 0:1])]
    else:
        states = [(c0_ref[i, 0], nm0_ref[i, 0, 0:1, :], nm0_ref[i, 0, 1:2, 0:1]) for i in range(n_inst)]
    outs, after = _mlstm_chunks(chunks, states, head, gn_ref[...], chained)
    for rs, h in zip(rows, outs):
        h_ref[rs, :] = h
    if chained:
        c, n, m = after[-1]
        c_sc[...] = c
        nm_sc[...] = pack_nm(n, m)

        @pl.when(step == pl.num_programs(2) - 1)
        def _():
            c_ref[0, 0] = c
            nm_ref[0, 0] = pack_nm(n, m)
    else:
        for i, (c, n, m) in enumerate(after):
            c_ref[i, 0] = c
            nm_ref[i, 0] = pack_nm(n, m)


def _mlstm_scan(q, k, v, og, ig, igt, gn, c0, nm0, n_seq, seq, length, n_inst, chained):
    dk, dv, nh = MLSTM_QK_DIM, MLSTM_V_DIM, MLSTM_HEADS
    rows = n_inst * length
    if chained:
        steps = seq // rows
        grid = (n_seq, nh, steps)
        r_idx = lambda b, h, s: b * steps + s
        sb = 1
    else:
        assert seq == length
        grid = (n_seq // n_inst, nh, 1)
        r_idx = lambda b, h, s: b
        sb = n_inst
    c_spec = pl.BlockSpec((sb, 1, dv, dk), lambda b, h, s: (b, h, 0, 0))
    nm_spec = pl.BlockSpec((sb, 1, 2, dk), lambda b, h, s: (b, h, 0, 0))
    return pl.pallas_call(
        functools.partial(_mlstm_scan_kernel, length=length, n_inst=n_inst, chained=chained),
        grid=grid,
        in_specs=[
            pl.BlockSpec((rows, dk), lambda b, h, s: (r_idx(b, h, s), h)),
            pl.BlockSpec((rows, dk), lambda b, h, s: (r_idx(b, h, s), h)),
            pl.BlockSpec((rows, dv), lambda b, h, s: (r_idx(b, h, s), h)),
            pl.BlockSpec((rows, dv), lambda b, h, s: (r_idx(b, h, s), h)),
            pl.BlockSpec((rows, 2 * nh), lambda b, h, s: (r_idx(b, h, s), 0)),
            pl.BlockSpec((2 * nh, rows), lambda b, h, s: (0, r_idx(b, h, s))),
            pl.BlockSpec((1, dv), lambda b, h, s: (0, h)),
            c_spec, nm_spec,
        ],
        out_specs=[pl.BlockSpec((rows, dv), lambda b, h, s: (r_idx(b, h, s), h)), c_spec, nm_spec],
        out_shape=[jax.ShapeDtypeStruct(v.shape, F32), jax.ShapeDtypeStruct((n_seq, nh, dv, dk), F32),
                   jax.ShapeDtypeStruct((n_seq, nh, 2, dk), F32)],
        scratch_shapes=[pltpu.VMEM((dv, dk), F32), pltpu.VMEM((2, dk), F32)],
        compiler_params=_params("parallel", "parallel", "arbitrary"),
        name="mlstm_scan_prompt" if chained else "mlstm_scan_sample",
    )(q, k, v, og, ig, igt, gn, c0, nm0)


def _mlstm_layer(x, g, c_state, n_state, m_state, w_in, b_if, gn, w_out, dims):
    batch, seq, dec_batch, dec_seq = dims
    dk, dv, nh = MLSTM_QK_DIM, MLSTM_V_DIM, MLSTM_HEADS
    d = w_in.shape[0]
    n_main = 2 * nh * dk + 2 * nh * dv
    w_main = w_in[:, :n_main].astype(BF16)
    w_g = jnp.pad(w_in[:, n_main:], ((0, 0), (0, LANES - 2 * nh))).astype(BF16)
    b_g = jnp.pad(b_if, (0, LANES - 2 * nh)).reshape(1, LANES)
    outs = [(nh * dk, F32, False), (nh * dk, F32, False), (nh * dv, F32, False), (nh * dv, F32, False),
            (2 * nh, F32, False), (2 * nh, F32, True)]
    q, k, v, og, ig, igt = _row_call(_mlstm_proj_body, [x], [g.reshape(1, d), w_main, w_g, b_g], outs, "mlstm_proj")
    gn2 = gn.reshape(1, nh * dv)
    zeros_c = jnp.zeros((batch, nh, dv, dk), F32)
    zeros_nm = jnp.zeros((batch, nh, 2, dk), F32)
    h_p, c_p, nm_p = _mlstm_scan(q[0], k[0], v[0], og[0], ig[0], igt[0], gn2, zeros_c, zeros_nm, batch, seq,
                                 MLSTM_CHUNK, MLSTM_CHUNKS_PER_STEP, True)
    extra = SUBLANES - dec_seq
    pad_rows = lambda t: jnp.pad(t.reshape(dec_batch, dec_seq, -1), ((0, 0), (0, extra), (0, 0))).reshape(dec_batch * SUBLANES, -1)
    ig_s = ig[1].reshape(dec_batch, dec_seq, 2 * nh)
    pad_gate = jnp.concatenate([jnp.full((dec_batch, extra, nh), MLSTM_NO_INPUT, F32), jnp.zeros((dec_batch, extra, nh), F32)], axis=-1)
    ig_s = jnp.concatenate([ig_s, pad_gate], axis=1).reshape(dec_batch * SUBLANES, 2 * nh)
    nm_s = jnp.stack([n_state, jnp.broadcast_to(m_state[..., None], n_state.shape)], axis=2)
    h_s, c_s, nm_s = _mlstm_scan(pad_rows(q[1]), pad_rows(k[1]), pad_rows(v[1]), pad_rows(og[1]), ig_s, ig_s.T, gn2,
                                 c_state, nm_s, dec_batch, SUBLANES, SUBLANES, MLSTM_SEQS_PER_STEP, False)
    h_s = h_s.reshape(dec_batch, SUBLANES, nh * dv)[:, :dec_seq].reshape(-1, nh * dv)
    state = (c_p, nm_p[:, :, 0], nm_p[:, :, 1, 0], c_s, nm_s[:, :, 0], nm_s[:, :, 1, 0])
    return ([(h_p, h_s)], w_out), state


N_MIXERS = 3


def kernel(x_prompt, x_sample,
           cache_k_l0, cache_v_l0, cache_f_l0,
           state_shift_l1, state_wkv_l1,
           state_c_l2, state_n_l2, state_m_l2,
           cache_k_l3, cache_v_l3, cache_f_l3,
           page_table,
           norm_g, final_g, ffn_w_in, ffn_w_out,
           fox_w_in, fox_b_f, fox_gq, fox_gk, fox_w_out,
           rw_mu, rw_w0, rw_w1, rw_w2, rw_a0, rw_a1, rw_a2, rw_g1, rw_g2,
           rw_kk, rw_ka, rw_rk, rw_w_rkv, rw_w_o, rw_gn_w, rw_gn_b,
           ml_w_in, ml_b_if, ml_gn, ml_w_out):
    batch, seq, d = x_prompt.shape
    dec_batch, dec_seq, _ = x_sample.shape
    dims = (batch, seq, dec_batch, dec_seq)
    depth = norm_g.shape[0]
    x = (x_prompt.reshape(batch * seq, d), x_sample.reshape(dec_batch * dec_seq, d))
    fox_caches = ((cache_k_l0, cache_v_l0, cache_f_l0), (cache_k_l3, cache_v_l3, cache_f_l3))
    rwkv_states = ((state_shift_l1, state_wkv_l1),)
    mlstm_states = ((state_c_l2, state_n_l2, state_m_l2),)
    w_in_bf, w_out_bf = ffn_w_in.astype(BF16), ffn_w_out.astype(BF16)
    layer_state = []
    for i in range(depth):
        kind, j = i % N_MIXERS, i // N_MIXERS
        x = _ffn(x, norm_g[i, 0], w_in_bf[i, 0], w_out_bf[i, 0])
        if kind == 0:
            ck, cv, cf = fox_caches[j]
            mixer, st = _fox_layer(x, norm_g[i, 1], fox_w_in[j], fox_b_f[j], fox_gq[j], fox_gk[j], fox_w_out[j],
                                   ck, cv, cf, page_table, dims)
        elif kind == 1:
            sh_in, wkv_in = rwkv_states[j]
            mixer, st = _rwkv_layer(x, norm_g[i, 1], sh_in, wkv_in, rw_mu[j], rw_w0[j], rw_w1[j], rw_w2[j], rw_a0[j],
                                    rw_a1[j], rw_a2[j], rw_g1[j], rw_g2[j], rw_kk[j], rw_ka[j], rw_rk[j], rw_w_rkv[j],
                                    rw_w_o[j], rw_gn_w[j], rw_gn_b[j], dims)
        else:
            c_in, n_in, m_in = mlstm_states[j]
            mixer, st = _mlstm_layer(x, norm_g[i, 1], c_in, n_in, m_in, ml_w_in[j], ml_b_if[j], ml_gn[j],
                                     ml_w_out[j], dims)
        layer_state.append(st)
        x = _ffn(x, norm_g[i, 2], w_in_bf[i, 1], w_out_bf[i, 1], mixer, final_g if i == depth - 1 else None)
    y_p, y_s = x
    out = (y_p.reshape(batch, seq, d), y_s.reshape(dec_batch, dec_seq, d))
    for st in layer_state:
        out = out + tuple(st)
    return out
```
